```python
import math
import jax, jax.numpy as jnp
from jax import lax
import numpy as np

D_MODEL = 1024
BATCH = 32
SEQ = 256
DEPTH = 4
DEC_BATCH = 8
DEC_SEQ = 4096
PAST_LEN = 256

GRID_W = 64
N_MIXERS = 2
N_LRU_LAYERS = (DEPTH + 1) // 2
N_SG_LAYERS = DEPTH // 2
LRU_WIDTH = (4 * D_MODEL // 3) // 128 * 128
LRU_HEADS = LRU_WIDTH // 128
LRU_BLOCK = LRU_WIDTH // LRU_HEADS
LRU_C = 8.0
CONV_W = 4
CHUNK = 128
SG_WIDTH = 2 * D_MODEL
SG_GROUPS = 8
SG_GROUP_DIM = SG_WIDTH // SG_GROUPS
N_EXPERTS = 16
EC_FACTOR = 2
EXPERT_FF = D_MODEL
RMS_EPS = 1e-6
POS_BASE = 10000.0

kernel_name = 'hybrid_rglru_sgmlp_ec_diffusion_step'


def _rmsnorm(x, g):
    xf = x.astype(jnp.float32)
    y = xf * lax.rsqrt(jnp.mean(xf * xf, axis=-1, keepdims=True) + RMS_EPS)
    return (y * g.astype(jnp.float32)).astype(x.dtype)


def _grid_pos_embed(n_tokens, dtype):
    rows = n_tokens // GRID_W
    row = jnp.repeat(jnp.arange(rows, dtype=jnp.float32), GRID_W)
    col = jnp.tile(jnp.arange(GRID_W, dtype=jnp.float32), rows)
    q = D_MODEL // 4
    freq = jnp.exp(-math.log(POS_BASE) * jnp.arange(q, dtype=jnp.float32) / q)
    ang_r = row[:, None] * freq
    ang_c = col[:, None] * freq
    emb = jnp.concatenate([jnp.sin(ang_r), jnp.cos(ang_r), jnp.sin(ang_c), jnp.cos(ang_c)], axis=-1)
    return emb.astype(dtype)


def _centred_depthwise_conv(x, w, b):
    t = x.shape[1]
    left = CONV_W // 2
    xp = jnp.pad(x, ((0, 0), (left, CONV_W - 1 - left), (0, 0)))
    return sum(xp[:, k:k + t] * w[k] for k in range(CONV_W)) + b


def _linear_scan(a, u, h0, reverse):
    if reverse:
        a = jnp.flip(a, axis=1)
        u = jnp.flip(u, axis=1)

    def combine(lhs, rhs):
        a_l, u_l = lhs
        a_r, u_r = rhs
        return a_l * a_r, a_r * u_l + u_r

    a_cum, u_cum = lax.associative_scan(combine, (a, u), axis=1)
    h = u_cum + a_cum * h0[:, None, :]
    final = h[:, -1]
    if reverse:
        h = jnp.flip(h, axis=1)
    return h, final


def _rglru_mixer(h, h0, w_in, conv_w, conv_b, w_a, b_a, w_x, b_x, lam, w_out):
    bsz, t, _ = h.shape
    proj = h @ w_in
    gate = jax.nn.gelu(proj[..., :LRU_WIDTH])
    xb = _centred_depthwise_conv(proj[..., LRU_WIDTH:], conv_w, conv_b)
    xh = xb.reshape(bsz, t, LRU_HEADS, LRU_BLOCK)
    hs, finals = [], []
    for d in range(2):
        r = jax.nn.sigmoid(jnp.einsum('bthi,hij->bthj', xh, w_a[d]).reshape(bsz, t, LRU_WIDTH) + b_a[d])
        i = jax.nn.sigmoid(jnp.einsum('bthi,hij->bthj', xh, w_x[d]).reshape(bsz, t, LRU_WIDTH) + b_x[d])
        log_a = -LRU_C * r.astype(jnp.float32) * jax.nn.softplus(-lam[d].astype(jnp.float32))
        a = jnp.exp(log_a)
        u = jnp.sqrt(-jnp.expm1(2.0 * log_a)) * (i * xb).astype(jnp.float32)
        h_d, fin = _linear_scan(a, u, h0[:, d].astype(jnp.float32), reverse=(d == 1))
        hs.append(h_d)
        finals.append(fin)
    y = ((hs[0] + hs[1]).astype(h.dtype) * gate) @ w_out
    return y, jnp.stack(finals, axis=1)


def _sgu_mixer(h, w_in, norm_g, w_s, b_s, w_out):
    bsz, t, _ = h.shape
    proj = jax.nn.gelu(h @ w_in)
    u = proj[..., :SG_WIDTH]
    v = _rmsnorm(proj[..., SG_WIDTH:], norm_g).reshape(bsz, t // CHUNK, CHUNK, SG_GROUPS, SG_GROUP_DIM)
    sv = jnp.einsum('gpq,bnqgc->bnpgc', w_s, v) + b_s.T[None, None, :, :, None]
    return (u * sv.reshape(bsz, t, SG_WIDTH)) @ w_out


def _ec_moe(x, router, w_gate, w_up, w_down):
    bsz, t, d = x.shape
    xf = x.reshape(bsz * t, d)
    cap = EC_FACTOR * (bsz * t) // N_EXPERTS
    aff = jax.nn.softmax(xf.astype(jnp.float32) @ router.astype(jnp.float32), axis=-1)
    g, idx = lax.top_k(aff.T, cap)
    xs = xf[idx]
    hh = jax.nn.silu(jnp.einsum('ecd,edf->ecf', xs, w_gate)) * jnp.einsum('ecd,edf->ecf', xs, w_up)
    out = jnp.einsum('ecf,efd->ecd', hh, w_down) * g[..., None].astype(x.dtype)
    y = jnp.zeros_like(xf).at[idx.reshape(-1)].add(out.reshape(-1, d))
    return y.reshape(bsz, t, d)


def _trunk(x, cond, h0, p):
    finals = []
    sc = jax.nn.silu(cond)
    for l in range(DEPTH):
        mod = (sc @ p['w_mod'][l] + p['b_mod'][l])[:, None, :]
        sh1, s1, g1, sh2, s2, g2 = jnp.split(mod, 6, axis=-1)
        hn = _rmsnorm(x, p['norm1_g'][l]) * (1 + s1) + sh1
        j = l // N_MIXERS
        if l % N_MIXERS == 0:
            y, fin = _rglru_mixer(hn, h0[:, j], p['lru_w_in'][j], p['lru_conv_w'][j], p['lru_conv_b'][j],
                                  p['lru_w_a'][j], p['lru_b_a'][j], p['lru_w_x'][j], p['lru_b_x'][j],
                                  p['lru_lam'][j], p['lru_w_out'][j])
            finals.append(fin)
        else:
            y = _sgu_mixer(hn, p['sg_w_in'][j], p['sg_norm_g'][j], p['sg_w_s'][j], p['sg_b_s'][j],
                           p['sg_w_out'][j])
        x = x + g1 * y
        hn = _rmsnorm(x, p['norm2_g'][l]) * (1 + s2) + sh2
        x = x + g2 * _ec_moe(hn, p['moe_router'][l], p['moe_w_gate'][l], p['moe_w_up'][l], p['moe_w_down'][l])
    return _rmsnorm(x, p['final_norm_g']), jnp.stack(finals, axis=1)


def setup_inputs(seed: int = 0) -> dict:
    key = jax.random.key(seed)
    ks = jax.random.split(key, 28)
    nrm = jax.random.normal
    f = jnp.float32
    d = D_MODEL
    a0 = jax.random.uniform(ks[16], (N_LRU_LAYERS, 2, LRU_WIDTH), f, minval=0.9, maxval=0.999)
    s = a0 ** (1.0 / LRU_C)
    lam = jnp.log(s) - jnp.log1p(-s)
    return {
        'x_prompt': nrm(ks[0], (BATCH, SEQ, d), f),
        'x_sample': nrm(ks[1], (DEC_BATCH, DEC_SEQ, d), f),
        'state_lru': nrm(ks[2], (DEC_BATCH, N_LRU_LAYERS, 2, LRU_WIDTH), f),
        'c': nrm(ks[3], (DEC_BATCH, d), f),
        'c_ctx': nrm(ks[4], (d,), f),
        'norm1_g': 1.0 + 0.01 * nrm(ks[5], (DEPTH, d), f),
        'norm2_g': 1.0 + 0.01 * nrm(ks[6], (DEPTH, d), f),
        'w_mod': nrm(ks[7], (DEPTH, d, 6 * d), f) * (0.25 * d ** -0.5),
        'b_mod': 0.01 * nrm(ks[8], (DEPTH, 6 * d), f),
        'lru_w_in': nrm(ks[9], (N_LRU_LAYERS, d, 2 * LRU_WIDTH), f) * d ** -0.5,
        'lru_conv_w': nrm(ks[10], (N_LRU_LAYERS, CONV_W, LRU_WIDTH), f) * CONV_W ** -0.5,
        'lru_conv_b': 0.01 * nrm(ks[11], (N_LRU_LAYERS, LRU_WIDTH), f),
        'lru_w_a': nrm(ks[12], (N_LRU_LAYERS, 2, LRU_HEADS, LRU_BLOCK, LRU_BLOCK), f) * LRU_BLOCK ** -0.5,
        'lru_b_a': 0.01 * nrm(ks[13], (N_LRU_LAYERS, 2, LRU_WIDTH), f),
        'lru_w_x': nrm(ks[14], (N_LRU_LAYERS, 2, LRU_HEADS, LRU_BLOCK, LRU_BLOCK), f) * LRU_BLOCK ** -0.5,
        'lru_b_x': 0.01 * nrm(ks[15], (N_LRU_LAYERS, 2, LRU_WIDTH), f),
        'lru_lam': lam,
        'lru_w_out': nrm(ks[17], (N_LRU_LAYERS, LRU_WIDTH, d), f) * LRU_WIDTH ** -0.5,
        'sg_w_in': nrm(ks[18], (N_SG_LAYERS, d, 2 * SG_WIDTH), f) * d ** -0.5,
        'sg_norm_g': 1.0 + 0.01 * nrm(ks[19], (N_SG_LAYERS, SG_WIDTH), f),
        'sg_w_s': nrm(ks[20], (N_SG_LAYERS, SG_GROUPS, CHUNK, CHUNK), f) * CHUNK ** -0.5,
        'sg_b_s': 1.0 + 0.1 * nrm(ks[21], (N_SG_LAYERS, SG_GROUPS, CHUNK), f),
        'sg_w_out': nrm(ks[22], (N_SG_LAYERS, SG_WIDTH, d), f) * SG_WIDTH ** -0.5,
        'moe_router': nrm(ks[23], (DEPTH, d, N_EXPERTS), f) * d ** -0.5,
        'moe_w_gate': nrm(ks[24], (DEPTH, N_EXPERTS, d, EXPERT_FF), f) * d ** -0.5,
        'moe_w_up': nrm(ks[25], (DEPTH, N_EXPERTS, d, EXPERT_FF), f) * d ** -0.5,
        'moe_w_down': nrm(ks[26], (DEPTH, N_EXPERTS, EXPERT_FF, d), f) * EXPERT_FF ** -0.5,
        'final_norm_g': 1.0 + 0.01 * nrm(ks[27], (d,), f),
    }


def reference(x_prompt, x_sample, state_lru, c, c_ctx, norm1_g, norm2_g, w_mod, b_mod,
              lru_w_in, lru_conv_w, lru_conv_b, lru_w_a, lru_b_a, lru_w_x, lru_b_x, lru_lam, lru_w_out,
              sg_w_in, sg_norm_g, sg_w_s, sg_b_s, sg_w_out,
              moe_router, moe_w_gate, moe_w_up, moe_w_down, final_norm_g):
    p = {
        'norm1_g': norm1_g, 'norm2_g': norm2_g, 'w_mod': w_mod, 'b_mod': b_mod,
        'lru_w_in': lru_w_in, 'lru_conv_w': lru_conv_w, 'lru_conv_b': lru_conv_b,
        'lru_w_a': lru_w_a, 'lru_b_a': lru_b_a, 'lru_w_x': lru_w_x, 'lru_b_x': lru_b_x,
        'lru_lam': lru_lam, 'lru_w_out': lru_w_out,
        'sg_w_in': sg_w_in, 'sg_norm_g': sg_norm_g, 'sg_w_s': sg_w_s, 'sg_b_s': sg_b_s, 'sg_w_out': sg_w_out,
        'moe_router': moe_router, 'moe_w_gate': moe_w_gate, 'moe_w_up': moe_w_up, 'moe_w_down': moe_w_down,
        'final_norm_g': final_norm_g,
    }
    h0_ctx = jnp.zeros((x_prompt.shape[0], N_LRU_LAYERS, 2, LRU_WIDTH), jnp.float32)
    y_prompt, new_state_lru = _trunk(x_prompt, c_ctx[None, :], h0_ctx, p)
    xs = x_sample + _grid_pos_embed(x_sample.shape[1], x_sample.dtype)[None]
    y_sample, _ = _trunk(xs, c, state_lru, p)
    return (y_prompt, y_sample, new_state_lru)
```

```python
import functools
import math

import jax
import jax.numpy as jnp
from jax import lax
from jax.experimental import pallas as pl
from jax.experimental.pallas import tpu as pltpu

F32 = jnp.float32
BF16 = jnp.bfloat16
I32 = jnp.int32
HIGHEST = lax.Precision.HIGHEST

RMS_EPS = 1e-6
LRU_C = 8.0
CONV_W = 4
CHUNK = 128
SG_GROUPS = 8
GRID_W = 64
POS_BASE = 10000.0
EC_FACTOR = 2
LANE = 128
SUBLANE = 8
BF16_ROWS = 16
VMEM_LIMIT = 56 * 1024 * 1024

TOKEN_TILE = 256
SLOT_BLOCK = 256
SLOT_WINDOW = 128


def _cparams(sem):
    return pltpu.CompilerParams(dimension_semantics=sem, vmem_limit_bytes=VMEM_LIMIT)


def _rms(x, g):
    return x * lax.rsqrt(jnp.mean(x * x, axis=-1, keepdims=True) + RMS_EPS) * g


def _gelu_tanh(x):
    c = math.sqrt(2.0 / math.pi)
    return x * (0.5 * (1.0 + jnp.tanh(c * (x + 0.044715 * (x * x * x)))))


def _sigmoid(x):
    return 1.0 / (1.0 + jnp.exp(-x))


def _log1p(e):
    w = 1.0 + e
    return jnp.where(w == 1.0, e, e * jnp.log(w) / jnp.where(w == 1.0, 1.0, w - 1.0))


def _softplus(x):
    return jnp.maximum(x, 0.0) + _log1p(jnp.exp(-jnp.abs(x)))


def _neg_expm1_2x(x):
    t = jnp.tanh(x)
    return (-2.0 * t) / (1.0 - t)


def _mod_kernel(c_ref, w_ref, b_ref, o_ref):
    c = c_ref[...]
    sc = c * _sigmoid(c)
    o_ref[0] = jnp.dot(sc, w_ref[0], preferred_element_type=F32, precision=HIGHEST) + b_ref[0]


def _modulation(cond, w_mod, b_mod):
    L, D, D6 = w_mod.shape
    R = cond.shape[0]
    tn = D6 // 4
    return pl.pallas_call(
        _mod_kernel,
        grid=(L, D6 // tn),
        in_specs=[pl.BlockSpec((R, D), lambda l, n: (0, 0)),
                  pl.BlockSpec((1, D, tn), lambda l, n: (l, 0, n)),
                  pl.BlockSpec((1, 1, tn), lambda l, n: (l, 0, n))],
        out_specs=pl.BlockSpec((1, R, tn), lambda l, n: (l, 0, n)),
        out_shape=jax.ShapeDtypeStruct((L, R, D6), F32),
        compiler_params=_cparams(("arbitrary", "arbitrary")),
        name="modulation",
    )(cond, w_mod, b_mod.reshape(L, 1, D6))


def _add_kernel(x_ref, p_ref, o_ref):
    o_ref[0] = x_ref[0] + p_ref[...]


def _add_pos(x, pe):
    B, T, D = x.shape
    tt = min(T, 512)
    return pl.pallas_call(
        _add_kernel,
        grid=(T // tt, B),
        in_specs=[pl.BlockSpec((1, tt, D), lambda t, b: (b, t, 0)),
                  pl.BlockSpec((tt, D), lambda t, b: (t, 0))],
        out_specs=pl.BlockSpec((1, tt, D), lambda t, b: (b, t, 0)),
        out_shape=jax.ShapeDtypeStruct((B, T, D), F32),
        compiler_params=_cparams(("arbitrary", "arbitrary")),
        name="add_pos",
    )(x, pe)


def _scan8(a, u, reverse):
    n = a.shape[0]
    row = lax.broadcasted_iota(I32, a.shape, 0) & (SUBLANE - 1)
    for k in (1, 2, 4):
        if reverse:
            a_s, u_s, m = pltpu.roll(a, n - k, 0), pltpu.roll(u, n - k, 0), row < SUBLANE - k
        else:
            a_s, u_s, m = pltpu.roll(a, k, 0), pltpu.roll(u, k, 0), row >= k
        u = jnp.where(m, u + a * u_s, u)
        a = jnp.where(m, a * a_s, a)
    return a, u


def _lru_gates(xb, wg_ref, ba, bx, sp, a_scr, u_scr, reverse):
    heads = wg_ref.shape[0]
    for h in range(heads):
        hs = slice(h * LANE, (h + 1) * LANE)
        xh = xb[:, hs]
        z = jnp.dot(xh.astype(BF16), wg_ref[h], preferred_element_type=F32)
        r = _sigmoid(z[:, :LANE] + ba[:, hs])
        i = _sigmoid(z[:, LANE:] + bx[:, hs])
        log_a = (-LRU_C) * r * sp[:, hs]
        a = jnp.exp(log_a)
        u = jnp.sqrt(_neg_expm1_2x(log_a)) * (i * xh)
        a, u = _scan8(a, u, reverse)
        a_scr[:, hs] = a
        u_scr[:, hs] = u


def _lru_fwd_kernel(x_ref, xp_ref, xn_ref, mod_ref, g_ref, win_ref, cw_ref, cb_ref, wg_ref,
                    ba_ref, bx_ref, lam_ref, h0_ref,
                    gate_ref, xb_ref, hf_ref, fin_ref,
                    carry_ref, a_scr, u_scr, *, tt, n_t, width):
    t = pl.program_id(1)

    @pl.when(t == 0)
    def _():
        carry_ref[...] = h0_ref[0]

    ext = jnp.concatenate([xp_ref[0], x_ref[0], xn_ref[0]], axis=0)
    hn = _rms(ext, g_ref[...]) * (1.0 + mod_ref[0, 1:2, :]) + mod_ref[0, 0:1, :]
    proj = jnp.dot(hn.astype(BF16), win_ref[...], preferred_element_type=F32)
    gate_ref[0] = _gelu_tanh(proj[SUBLANE:SUBLANE + tt, :width])

    row = lax.broadcasted_iota(I32, (tt + 2 * SUBLANE, 1), 0)
    valid = ((row >= SUBLANE) | (t > 0)) & ((row < tt + SUBLANE) | (t < n_t - 1))
    xpre = jnp.where(valid, proj[:, width:], 0.0)
    left = CONV_W // 2
    xb = cb_ref[...]
    for k in range(CONV_W):
        s = SUBLANE - left + k
        xb = xb + cw_ref[k:k + 1, :] * xpre[s:s + tt, :]
    xb_ref[0] = xb

    sp = _softplus(-lam_ref[...])
    _lru_gates(xb, wg_ref, ba_ref[...], bx_ref[...], sp, a_scr, u_scr, reverse=False)

    def body(gi, carry):
        r0 = pl.multiple_of(gi * SUBLANE, SUBLANE)
        h8 = u_scr[pl.ds(r0, SUBLANE), :] + a_scr[pl.ds(r0, SUBLANE), :] * carry
        hf_ref[0, pl.ds(r0, SUBLANE), :] = h8
        return h8[SUBLANE - 1:SUBLANE, :]

    carry = lax.fori_loop(0, tt // SUBLANE, body, carry_ref[...])
    carry_ref[...] = carry
    fin_ref[0] = carry


def _lru_bwd_kernel(x_ref, gate_ref, xb_ref, hf_ref, mod_ref, wg_ref, ba_ref, bx_ref, lam_ref, h0_ref,
                    wout_ref, g2_ref, rt_ref,
                    xo_ref, hn_ref, lg_ref, fin_ref,
                    carry_ref, a_scr, u_scr, hb_scr, *, tt):
    t = pl.program_id(1)

    @pl.when(t == 0)
    def _():
        carry_ref[...] = h0_ref[0]

    xb = xb_ref[0]
    sp = _softplus(-lam_ref[...])
    _lru_gates(xb, wg_ref, ba_ref[...], bx_ref[...], sp, a_scr, u_scr, reverse=True)

    n_g = tt // SUBLANE

    def body(gi, carry):
        r0 = pl.multiple_of((n_g - 1 - gi) * SUBLANE, SUBLANE)
        h8 = u_scr[pl.ds(r0, SUBLANE), :] + a_scr[pl.ds(r0, SUBLANE), :] * carry
        hb_scr[pl.ds(r0, SUBLANE), :] = h8
        return h8[0:1, :]

    carry = lax.fori_loop(0, n_g, body, carry_ref[...])
    carry_ref[...] = carry
    fin_ref[0] = carry

    y = ((hf_ref[0] + hb_scr[...]) * gate_ref[0]).astype(BF16)
    y = jnp.dot(y, wout_ref[...], preferred_element_type=F32)
    xo = x_ref[0] + mod_ref[0, 2:3, :] * y
    xo_ref[0] = xo
    _moe_prenorm(xo, mod_ref, g2_ref, rt_ref, hn_ref, lg_ref)


def _moe_prenorm(xo, mod_ref, g2_ref, rt_ref, hn_ref, lg_ref):
    hn = _rms(xo, g2_ref[...]) * (1.0 + mod_ref[0, 4:5, :]) + mod_ref[0, 3:4, :]
    hn_ref[0] = hn.astype(BF16)
    lg_ref[...] = lax.dot_general(rt_ref[...], hn, (((1,), (1,)), ((), ())),
                                  preferred_element_type=F32, precision=HIGHEST)


def _lru_tile(T):
    return min(T, 256)


def _lru_layer(x, mod, h0, p, j, norm1_g, norm2_g, router_t):
    B, T, D = x.shape
    W = p["lru_lam"].shape[-1]
    H = W // LANE
    E = router_t.shape[0]
    tt = _lru_tile(T)
    n_t = T // tt
    bc = mod.shape[0]
    mod_map = (lambda b, t: (b, 0, 0)) if bc > 1 else (lambda b, t: (0, 0, 0))
    n8 = T // SUBLANE
    r8 = tt // SUBLANE

    def wg(d):
        return jnp.concatenate([p["lru_w_a"][j, d], p["lru_w_x"][j, d]], axis=-1).astype(BF16)

    def vec(name, d):
        return p[name][j, d].reshape(1, W)

    full = lambda shape: pl.BlockSpec(shape, lambda b, t: (0,) * len(shape))
    tile_w = lambda imap: pl.BlockSpec((1, tt, W), imap)
    fwd_map = lambda b, t: (b, t, 0)
    gate, xb, hf, fin_f = pl.pallas_call(
        functools.partial(_lru_fwd_kernel, tt=tt, n_t=n_t, width=W),
        grid=(B, n_t),
        in_specs=[pl.BlockSpec((1, tt, D), fwd_map),
                  pl.BlockSpec((1, SUBLANE, D), lambda b, t: (b, jnp.maximum(t * r8 - 1, 0), 0)),
                  pl.BlockSpec((1, SUBLANE, D), lambda b, t: (b, jnp.minimum((t + 1) * r8, n8 - 1), 0)),
                  pl.BlockSpec((1, 6, D), mod_map),
                  full((1, D)), full((D, 2 * W)), full((CONV_W, W)), full((1, W)),
                  full((H, LANE, 2 * LANE)), full((1, W)), full((1, W)), full((1, W)),
                  pl.BlockSpec((1, 1, W), lambda b, t: (b, 0, 0))],
        out_specs=[tile_w(fwd_map), tile_w(fwd_map), tile_w(fwd_map),
                   pl.BlockSpec((1, 1, W), lambda b, t: (b, 0, 0))],
        out_shape=[jax.ShapeDtypeStruct((B, T, W), F32)] * 3 + [jax.ShapeDtypeStruct((B, 1, W), F32)],
        scratch_shapes=[pltpu.VMEM((1, W), F32), pltpu.VMEM((tt, W), F32), pltpu.VMEM((tt, W), F32)],
        compiler_params=_cparams(("arbitrary", "arbitrary")),
        name="lru_fwd",
    )(x, x, x, mod, norm1_g.reshape(1, D), p["lru_w_in"][j].astype(BF16), p["lru_conv_w"][j],
      p["lru_conv_b"][j].reshape(1, W), wg(0), vec("lru_b_a", 0), vec("lru_b_x", 0), vec("lru_lam", 0),
      h0[:, 0:1])

    bwd_map = lambda b, t: (b, n_t - 1 - t, 0)
    xo, hn2, lg, fin_b = pl.pallas_call(
        functools.partial(_lru_bwd_kernel, tt=tt),
        grid=(B, n_t),
        in_specs=[pl.BlockSpec((1, tt, D), bwd_map), tile_w(bwd_map), tile_w(bwd_map), tile_w(bwd_map),
                  pl.BlockSpec((1, 6, D), mod_map),
                  full((H, LANE, 2 * LANE)), full((1, W)), full((1, W)), full((1, W)),
                  pl.BlockSpec((1, 1, W), lambda b, t: (b, 0, 0)),
                  full((W, D)), full((1, D)), full((E, D))],
        out_specs=[pl.BlockSpec((1, tt, D), bwd_map), pl.BlockSpec((1, tt, D), bwd_map),
                   pl.BlockSpec((E, tt), lambda b, t: (0, b * n_t + n_t - 1 - t)),
                   pl.BlockSpec((1, 1, W), lambda b, t: (b, 0, 0))],
        out_shape=[jax.ShapeDtypeStruct((B, T, D), F32), jax.ShapeDtypeStruct((B, T, D), BF16),
                   jax.ShapeDtypeStruct((E, B * T), F32), jax.ShapeDtypeStruct((B, 1, W), F32)],
        scratch_shapes=[pltpu.VMEM((1, W), F32), pltpu.VMEM((tt, W), F32), pltpu.VMEM((tt, W), F32),
                        pltpu.VMEM((tt, W), F32)],
        compiler_params=_cparams(("arbitrary", "arbitrary")),
        name="lru_bwd",
    )(x, gate, xb, hf, mod, wg(1), vec("lru_b_a", 1), vec("lru_b_x", 1), vec("lru_lam", 1), h0[:, 1:2],
      p["lru_w_out"][j].astype(BF16), norm2_g.reshape(1, D), router_t)
    return xo, hn2, lg, jnp.concatenate([fin_f, fin_b], axis=1)


def _sgu_kernel(x_ref, mod_ref, g_ref, win_ref, ng_ref, ws_ref, bs_ref, wout_ref, g2_ref, rt_ref,
                xo_ref, hn_ref, lg_ref, u_scr, v_scr, p_scr, *, tt, sgw):
    x = x_ref[0]
    hn = (_rms(x, g_ref[...]) * (1.0 + mod_ref[0, 1:2, :]) + mod_ref[0, 0:1, :]).astype(BF16)
    u_scr[...] = _gelu_tanh(jnp.dot(hn, win_ref[:, :sgw], preferred_element_type=F32))
    v = _gelu_tanh(jnp.dot(hn, win_ref[:, sgw:], preferred_element_type=F32))
    v_scr[...] = _rms(v, ng_ref[...]).astype(BF16)
    gd = sgw // SG_GROUPS
    for n in range(tt // CHUNK):
        rs = slice(n * CHUNK, (n + 1) * CHUNK)
        for g in range(SG_GROUPS):
            cs = slice(g * gd, (g + 1) * gd)
            sv = jnp.dot(ws_ref[g], v_scr[rs, cs], preferred_element_type=F32) + bs_ref[:, g:g + 1]
            p_scr[rs, cs] = (u_scr[rs, cs] * sv).astype(BF16)
    y = jnp.dot(p_scr[...], wout_ref[...], preferred_element_type=F32)
    xo = x + mod_ref[0, 2:3, :] * y
    xo_ref[0] = xo
    _moe_prenorm(xo, mod_ref, g2_ref, rt_ref, hn_ref, lg_ref)


def _sgu_layer(x, mod, p, j, norm1_g, norm2_g, router_t):
    B, T, D = x.shape
    sgw = p["sg_norm_g"].shape[-1]
    E = router_t.shape[0]
    tt = min(T, 256)
    n_t = T // tt
    bc = mod.shape[0]
    mod_map = (lambda b, t: (b, 0, 0)) if bc > 1 else (lambda b, t: (0, 0, 0))
    full = lambda shape: pl.BlockSpec(shape, lambda b, t: (0,) * len(shape))
    tile = pl.BlockSpec((1, tt, D), lambda b, t: (b, t, 0))
    return pl.pallas_call(
        functools.partial(_sgu_kernel, tt=tt, sgw=sgw),
        grid=(B, n_t),
        in_specs=[tile, pl.BlockSpec((1, 6, D), mod_map), full((1, D)), full((D, 2 * sgw)), full((1, sgw)),
                  full((SG_GROUPS, CHUNK, CHUNK)), full((CHUNK, SG_GROUPS)), full((sgw, D)),
                  full((1, D)), full((E, D))],
        out_specs=[tile, tile, pl.BlockSpec((E, tt), lambda b, t: (0, b * n_t + t))],
        out_shape=[jax.ShapeDtypeStruct((B, T, D), F32), jax.ShapeDtypeStruct((B, T, D), BF16),
                   jax.ShapeDtypeStruct((E, B * T), F32)],
        scratch_shapes=[pltpu.VMEM((tt, sgw), F32), pltpu.VMEM((tt, sgw), BF16), pltpu.VMEM((tt, sgw), BF16)],
        compiler_params=_cparams(("arbitrary", "arbitrary")),
        name="sgu",
    )(x, mod, norm1_g.reshape(1, D), p["sg_w_in"][j].astype(BF16), p["sg_norm_g"][j].reshape(1, sgw),
      p["sg_w_s"][j].astype(BF16), p["sg_b_s"][j].T, p["sg_w_out"][j].astype(BF16),
      norm2_g.reshape(1, D), router_t)


def _select_kernel(lg_ref, pos_ref, g_ref, off_ref, aff_scr, *, cap, tb):
    E, N = lg_ref.shape
    n_tile = N // tb
    lg = lg_ref[...]
    ex = jnp.exp(lg - jnp.max(lg, axis=0, keepdims=True))
    aff_scr[...] = ex / jnp.sum(ex, axis=0, keepdims=True)

    def search(it, cur):
        cand = cur | (1 << (30 - it))
        bits = pltpu.bitcast(aff_scr[...], I32)
        cnt = jnp.sum(jnp.where(bits >= cand, 1.0, 0.0), axis=1, keepdims=True)
        return jnp.where(cnt >= cap, cand, cur)

    thr = lax.fori_loop(0, 31, search, jnp.zeros((E, 1), I32))
    bits = pltpu.bitcast(aff_scr[...], I32)
    n_gt = jnp.sum(jnp.where(bits > thr, 1.0, 0.0), axis=1, keepdims=True)
    need = cap - n_gt

    tri = (lax.broadcasted_iota(I32, (tb, tb), 0) <= lax.broadcasted_iota(I32, (tb, tb), 1)).astype(BF16)
    lane = lax.broadcasted_iota(I32, off_ref.shape, 1)

    def chunk(c, carry):
        c_eq, c_pos = carry
        l0 = pl.multiple_of(c * tb, tb)
        aff = aff_scr[:, pl.ds(l0, tb)]
        b = pltpu.bitcast(aff, I32)
        eq = b == thr
        eqf = jnp.where(eq, 1.0, 0.0)
        rank = jnp.dot(eqf.astype(BF16), tri, preferred_element_type=F32) - eqf + c_eq
        sel = (b > thr) | (eq & (rank < need))
        self_ = jnp.where(sel, 1.0, 0.0)
        inc = jnp.dot(self_.astype(BF16), tri, preferred_element_type=F32)
        pos = inc - self_ + c_pos
        pos_ref[:, pl.ds(l0, tb)] = jnp.where(sel, pos, -1.0).astype(I32)
        g_ref[:, pl.ds(l0, tb)] = jnp.where(sel, aff, 0.0)
        off_ref[...] = jnp.where(lane == c, jnp.broadcast_to(c_pos, off_ref.shape).astype(I32), off_ref[...])
        return (c_eq + jnp.sum(eqf, axis=1, keepdims=True), c_pos + inc[:, tb - 1:tb])

    off_ref[...] = jnp.zeros(off_ref.shape, I32)
    zero = jnp.zeros((E, 1), F32)
    _, total = lax.fori_loop(0, n_tile, chunk, (zero, zero))
    off_ref[...] = jnp.where(lane == n_tile, jnp.broadcast_to(total, off_ref.shape).astype(I32), off_ref[...])


def _select(logits_t, cap, tb):
    E, N = logits_t.shape
    offw = -(-(N // tb + 1) // LANE) * LANE
    return pl.pallas_call(
        functools.partial(_select_kernel, cap=cap, tb=tb),
        out_shape=[jax.ShapeDtypeStruct((E, N), I32), jax.ShapeDtypeStruct((E, N), F32),
                   jax.ShapeDtypeStruct((E, offw), I32)],
        scratch_shapes=[pltpu.VMEM((E, N), F32)],
        compiler_params=pltpu.CompilerParams(vmem_limit_bytes=VMEM_LIMIT),
        name="moe_select",
    )(logits_t)


def _expert_kernel(off_ref, x_ref, pos_ref, g_ref, wg_ref, wu_ref, wd_ref, y_ref,
                   xs_scr, gs_scr, *, n_exp, tb, sb, ws, offw):
    s = pl.program_id(0)
    d = pl.program_id(1)
    sub = x_ref.shape[0] // tb

    @pl.when(s < n_exp)
    def _gather():
        slot = s % 2

        @pl.when(d == 0)
        def _():
            xs_scr[slot] = jnp.zeros(xs_scr.shape[1:], xs_scr.dtype)
            gs_scr[slot] = jnp.zeros(gs_scr.shape[1:], gs_scr.dtype)

        for j in range(sub):
            i = d * sub + j
            o0 = off_ref[s * offw + i]
            cnt = off_ref[s * offw + i + 1] - o0
            base = (o0 // BF16_ROWS) * BF16_ROWS
            n_ch = jnp.where(cnt > 0, (o0 - base + cnt + ws - 1) // ws, 0)
            l0 = pl.multiple_of(i * tb, tb)
            prow = pos_ref[0, :, pl.ds(l0, tb)]
            grow = g_ref[0, :, pl.ds(l0, tb)]
            xt = x_ref[j * tb:(j + 1) * tb, :]

            def chunk(ch, _):
                r0 = pl.multiple_of(base + ch * ws, BF16_ROWS)
                hit = prow == (r0 + lax.broadcasted_iota(I32, (ws, tb), 0))
                rows = jnp.dot(jnp.where(hit, 1.0, 0.0).astype(BF16), xt, preferred_element_type=F32)
                xs_scr[slot, pl.ds(r0, ws), :] += rows.astype(BF16)
                gcol = jnp.sum(jnp.where(hit, grow, 0.0), axis=1, keepdims=True)
                gs_scr[slot, pl.ds(r0, ws), :] += jnp.broadcast_to(gcol, (ws, LANE))
                return 0

            lax.fori_loop(0, n_ch, chunk, 0)

    @pl.when(s == 0)
    def _():
        y_ref[...] = jnp.zeros(y_ref.shape, y_ref.dtype)

    @pl.when(s > 0)
    def _ffn():
        slot = (s - 1) % 2
        r0 = pl.multiple_of(d * sb, sb)
        xb = xs_scr[slot, pl.ds(r0, sb), :]
        hg = jnp.dot(xb, wg_ref[0, 0], preferred_element_type=F32)
        hu = jnp.dot(xb, wu_ref[0, 0], preferred_element_type=F32)
        h = (hg * _sigmoid(hg) * hu).astype(BF16)
        o = jnp.dot(h, wd_ref[0, 0], preferred_element_type=F32)
        y_ref[...] = (o * gs_scr[slot, pl.ds(r0, sb), 0:1]).astype(BF16)


def _experts(hn2, pos, g, off, w_gate, w_up, w_down, layer, cap, tb, sb, ws):
    N, D = hn2.shape
    E = pos.shape[0]
    F = w_gate.shape[-1]
    offw = off.shape[1]
    n_blk = cap // sb
    ns = N // n_blk
    assert ns % tb == 0 and cap % sb == 0
    emap = lambda s, d, off: (jnp.minimum(s, E - 1), 0, 0)
    wmap = lambda s, d, off: (layer, jnp.maximum(s - 1, 0), 0, 0)
    grid_spec = pltpu.PrefetchScalarGridSpec(
        num_scalar_prefetch=1,
        grid=(E + 1, n_blk),
        in_specs=[pl.BlockSpec((ns, D), lambda s, d, off: (d, 0)),
                  pl.BlockSpec((1, 1, N), emap), pl.BlockSpec((1, 1, N), emap),
                  pl.BlockSpec((1, 1, D, F), wmap), pl.BlockSpec((1, 1, D, F), wmap),
                  pl.BlockSpec((1, 1, F, D), wmap)],
        out_specs=pl.BlockSpec((sb, D), lambda s, d, off: (s * n_blk + d, 0)),
        scratch_shapes=[pltpu.VMEM((2, cap + ws, D), BF16), pltpu.VMEM((2, cap + ws, LANE), F32)],
    )
    return pl.pallas_call(
        functools.partial(_expert_kernel, n_exp=E, tb=tb, sb=sb, ws=ws, offw=offw),
        grid_spec=grid_spec,
        out_shape=jax.ShapeDtypeStruct(((E + 1) * cap, D), BF16),
        compiler_params=_cparams(("arbitrary", "arbitrary")),
        name="moe_experts",
    )(off.reshape(-1), hn2, pos.reshape(E, 1, N), g.reshape(E, 1, N), w_gate, w_up, w_down)


def _window_base(off_ref, e, i, offw, cap, ws):
    o0 = off_ref[e * offw + i]
    return jnp.minimum((o0 // BF16_ROWS) * BF16_ROWS, cap - ws)


def _combine_kernel(off_ref, x_ref, pos_ref, mod_ref, y_hbm, xo_ref, win, extra, sem, xsem,
                    *, n_exp, n_tile, cap, ws, offw):
    i = pl.program_id(0)
    slot = i % 2
    tb = x_ref.shape[0]

    def window_copy(ti, sl, e):
        base = _window_base(off_ref, e, ti, offw, cap, ws)
        return pltpu.make_async_copy(y_hbm.at[pl.ds((e + 1) * cap + base, ws), :], win.at[sl, e], sem.at[sl, e])

    @pl.when(i == 0)
    def _():
        for e in range(n_exp):
            window_copy(0, 0, e).start()

    @pl.when(i + 1 < n_tile)
    def _():
        for e in range(n_exp):
            window_copy(i + 1, 1 - slot, e).start()

    post = jnp.transpose(pos_ref[...].astype(F32)).astype(I32)
    lane = lax.broadcasted_iota(I32, (tb, ws), 1)
    acc = jnp.zeros(xo_ref.shape, F32)
    for e in range(n_exp):
        end = off_ref[e * offw + i + 1]
        base = _window_base(off_ref, e, i, offw, cap, ws)
        pc = post[:, e:e + 1]
        window_copy(i, slot, e).wait()
        hit = pc == (base + lane)
        acc = acc + jnp.dot(jnp.where(hit, 1.0, 0.0).astype(BF16), win[slot, e], preferred_element_type=F32)
        n_more = jnp.maximum(end - (base + ws) + ws - 1, 0) // ws

        def more(k, a):
            b2 = jnp.minimum(base + (k + 1) * ws, cap - ws)
            cp = pltpu.make_async_copy(y_hbm.at[pl.ds((e + 1) * cap + b2, ws), :], extra, xsem)
            cp.start()
            cp.wait()
            h2 = (pc == (b2 + lane)) & (pc >= base + (k + 1) * ws)
            return a + jnp.dot(jnp.where(h2, 1.0, 0.0).astype(BF16), extra[...], preferred_element_type=F32)

        acc = lax.fori_loop(0, n_more, more, acc)
    xo_ref[...] = x_ref[...] + mod_ref[0, 5:6, :] * acc


def _combine(x2, pos, off, y2, mod, tokens_per_batch, cap, tb, ws):
    N, D = x2.shape
    E = pos.shape[0]
    offw = off.shape[1]
    n_tile = N // tb
    per_b = tokens_per_batch // tb
    bc = mod.shape[0]
    mod_map = (lambda i, off: (i // per_b, 0, 0)) if bc > 1 else (lambda i, off: (0, 0, 0))
    grid_spec = pltpu.PrefetchScalarGridSpec(
        num_scalar_prefetch=1,
        grid=(n_tile,),
        in_specs=[pl.BlockSpec((tb, D), lambda i, off: (i, 0)),
                  pl.BlockSpec((E, tb), lambda i, off: (0, i)),
                  pl.BlockSpec((1, 6, D), mod_map),
                  pl.BlockSpec(memory_space=pl.ANY)],
        out_specs=pl.BlockSpec((tb, D), lambda i, off: (i, 0)),
        scratch_shapes=[pltpu.VMEM((2, E, ws, D), BF16), pltpu.VMEM((ws, D), BF16),
                        pltpu.SemaphoreType.DMA((2, E)), pltpu.SemaphoreType.DMA(())],
    )
    return pl.pallas_call(
        functools.partial(_combine_kernel, n_exp=E, n_tile=n_tile, cap=cap, ws=ws, offw=offw),
        grid_spec=grid_spec,
        out_shape=jax.ShapeDtypeStruct((N, D), F32),
        compiler_params=_cparams(("arbitrary",)),
        name="moe_combine",
    )(off.reshape(-1), x2, pos, mod, y2)


def _moe(x, hn2, logits_t, mod, p, layer):
    B, T, D = x.shape
    N = B * T
    E = logits_t.shape[0]
    cap = EC_FACTOR * N // E
    tb = min(TOKEN_TILE, T)
    sb = min(SLOT_BLOCK, cap)
    ws = min(SLOT_WINDOW, cap)
    pos, g, off = _select(logits_t, cap, tb)
    y2 = _experts(hn2.reshape(N, D), pos, g, off, p["moe_w_gate"], p["moe_w_up"], p["moe_w_down"],
                  layer, cap, tb, sb, ws)
    xo = _combine(x.reshape(N, D), pos, off, y2, mod, T, cap, tb, ws)
    return xo.reshape(B, T, D)


def _final_kernel(x_ref, g_ref, o_ref):
    o_ref[...] = _rms(x_ref[...], g_ref[...])


def _final_norm(x, g):
    B, T, D = x.shape
    N = B * T
    tn = min(N, 1024)
    return pl.pallas_call(
        _final_kernel,
        grid=(N // tn,),
        in_specs=[pl.BlockSpec((tn, D), lambda i: (i, 0)), pl.BlockSpec((1, D), lambda i: (0, 0))],
        out_specs=pl.BlockSpec((tn, D), lambda i: (i, 0)),
        out_shape=jax.ShapeDtypeStruct((N, D), F32),
        compiler_params=_cparams(("arbitrary",)),
        name="final_norm",
    )(x.reshape(N, D), g.reshape(1, D)).reshape(B, T, D)


def _grid_pos_embed(n_tokens, d_model):
    rows = n_tokens // GRID_W
    row = jnp.repeat(jnp.arange(rows, dtype=F32), GRID_W)
    col = jnp.tile(jnp.arange(GRID_W, dtype=F32), rows)
    q = d_model // 4
    freq = jnp.exp(-math.log(POS_BASE) * jnp.arange(q, dtype=F32) / q)
    ang_r = row[:, None] * freq
    ang_c = col[:, None] * freq
    return jnp.concatenate([jnp.sin(ang_r), jnp.cos(ang_r), jnp.sin(ang_c), jnp.cos(ang_c)], axis=-1)


def _trunk(x, mods, h0, p):
    depth = p["norm1_g"].shape[0]
    finals = []
    for l in range(depth):
        mod = mods[l]
        router_t = p["moe_router"][l].T
        j = l // 2
        if l % 2 == 0:
            x, hn2, lg, fin = _lru_layer(x, mod, h0[:, j], p, j, p["norm1_g"][l], p["norm2_g"][l], router_t)
            finals.append(fin)
        else:
            x, hn2, lg = _sgu_layer(x, mod, p, j, p["norm1_g"][l], p["norm2_g"][l], router_t)
        x = _moe(x, hn2, lg, mod, p, l)
    return _final_norm(x, p["final_norm_g"]), jnp.stack(finals, axis=1)


def kernel(x_prompt, x_sample, state_lru, c, c_ctx, norm1_g, norm2_g, w_mod, b_mod,
           lru_w_in, lru_conv_w, lru_conv_b, lru_w_a, lru_b_a, lru_w_x, lru_b_x, lru_lam, lru_w_out,
           sg_w_in, sg_norm_g, sg_w_s, sg_b_s, sg_w_out,
           moe_router, moe_w_gate, moe_w_up, moe_w_down, final_norm_g):
    p = dict(norm1_g=norm1_g, norm2_g=norm2_g, lru_w_in=lru_w_in, lru_conv_w=lru_conv_w,
             lru_conv_b=lru_conv_b, lru_w_a=lru_w_a, lru_b_a=lru_b_a, lru_w_x=lru_w_x, lru_b_x=lru_b_x,
             lru_lam=lru_lam, lru_w_out=lru_w_out, sg_w_in=sg_w_in, sg_norm_g=sg_norm_g, sg_w_s=sg_w_s,
             sg_b_s=sg_b_s, sg_w_out=sg_w_out, moe_router=moe_router, moe_w_gate=moe_w_gate.astype(BF16),
             moe_w_up=moe_w_up.astype(BF16), moe_w_down=moe_w_down.astype(BF16), final_norm_g=final_norm_g)
    L, D, _ = w_mod.shape
    bs = c.shape[0]
    n_lru, _, W = lru_lam.shape

    rows = -(-(1 + bs) // SUBLANE) * SUBLANE
    cond = jnp.zeros((rows, D), F32).at[0].set(c_ctx).at[1:1 + bs].set(c)
    mods = _modulation(cond, w_mod, b_mod).reshape(L, rows, 6, D)

    h0_ctx = jnp.zeros((x_prompt.shape[0], n_lru, 2, W), F32)
    y_prompt, new_state = _trunk(x_prompt, mods[:, 0:1], h0_ctx, p)

    xs = _add_pos(x_sample, _grid_pos_embed(x_sample.shape[1], D))
    y_sample, _ = _trunk(xs, mods[:, 1:1 + bs], state_lru, p)
    return (y_prompt, y_sample, new_state)
```

```python
import functools
import math

import jax
import jax.numpy as jnp
from jax import lax
from jax.experimental import pallas as pl
from jax.experimental.pallas import tpu as pltpu

F32 = jnp.float32
BF16 = jnp.bfloat16
I32 = jnp.int32
HIGHEST = lax.Precision.HIGHEST

RMS_EPS = 1e-6
LRU_C = 8.0
CONV_W = 4
CHUNK = 128
SG_GROUPS = 8
GRID_W = 64
POS_BASE = 10000.0
EC_FACTOR = 2
LANE = 128
SUBLANE = 8
BF16_ROWS = 16
VMEM_LIMIT = 56 * 1024 * 1024

TOKEN_TILE = 256
SLOT_BLOCK = 1024
GATHER_TOKENS = 2048
SLOT_WINDOW = 64


def _cparams(sem):
    return pltpu.CompilerParams(dimension_semantics=sem, vmem_limit_bytes=VMEM_LIMIT)


def _rms(x, g):
    return x * lax.rsqrt(jnp.mean(x * x, axis=-1, keepdims=True) + RMS_EPS) * g


def _gelu_tanh(x):
    c = math.sqrt(2.0 / math.pi)
    return x * (0.5 * (1.0 + jnp.tanh(c * (x + 0.044715 * (x * x * x)))))


def _sigmoid(x):
    return 1.0 / (1.0 + jnp.exp(-x))


def _log1p(e):
    w = 1.0 + e
    return jnp.where(w == 1.0, e, e * jnp.log(w) / jnp.where(w == 1.0, 1.0, w - 1.0))


def _softplus(x):
    return jnp.maximum(x, 0.0) + _log1p(jnp.exp(-jnp.abs(x)))


def _neg_expm1_2x(x):
    t = jnp.tanh(x)
    return (-2.0 * t) / (1.0 - t)


def _mod_kernel(c_ref, w_ref, b_ref, o_ref):
    c = c_ref[...]
    sc = c * _sigmoid(c)
    o_ref[0] = jnp.dot(sc, w_ref[0], preferred_element_type=F32, precision=HIGHEST) + b_ref[0]


def _modulation(cond, w_mod, b_mod):
    L, D, D6 = w_mod.shape
    R = cond.shape[0]
    tn = D6 // 4
    return pl.pallas_call(
        _mod_kernel,
        grid=(L, D6 // tn),
        in_specs=[pl.BlockSpec((R, D), lambda l, n: (0, 0)),
                  pl.BlockSpec((1, D, tn), lambda l, n: (l, 0, n)),
                  pl.BlockSpec((1, 1, tn), lambda l, n: (l, 0, n))],
        out_specs=pl.BlockSpec((1, R, tn), lambda l, n: (l, 0, n)),
        out_shape=jax.ShapeDtypeStruct((L, R, D6), F32),
        compiler_params=_cparams(("arbitrary", "arbitrary")),
        name="modulation",
    )(cond, w_mod, b_mod.reshape(L, 1, D6))


def _add_kernel(x_ref, p_ref, o_ref):
    o_ref[0] = x_ref[0] + p_ref[...]


def _add_pos(x, pe):
    B, T, D = x.shape
    tt = min(T, 512)
    return pl.pallas_call(
        _add_kernel,
        grid=(T // tt, B),
        in_specs=[pl.BlockSpec((1, tt, D), lambda t, b: (b, t, 0)),
                  pl.BlockSpec((tt, D), lambda t, b: (t, 0))],
        out_specs=pl.BlockSpec((1, tt, D), lambda t, b: (b, t, 0)),
        out_shape=jax.ShapeDtypeStruct((B, T, D), F32),
        compiler_params=_cparams(("arbitrary", "arbitrary")),
        name="add_pos",
    )(x, pe)


def _scan8(a, u, reverse):
    n = a.shape[0]
    row = lax.broadcasted_iota(I32, a.shape, 0) & (SUBLANE - 1)
    for k in (1, 2, 4):
        if reverse:
            a_s, u_s, m = pltpu.roll(a, n - k, 0), pltpu.roll(u, n - k, 0), row < SUBLANE - k
        else:
            a_s, u_s, m = pltpu.roll(a, k, 0), pltpu.roll(u, k, 0), row >= k
        u = jnp.where(m, u + a * u_s, u)
        a = jnp.where(m, a * a_s, a)
    return a, u


def _lru_gates(xb, wg_ref, ba, bx, sp, a_scr, u_scr, reverse):
    heads = wg_ref.shape[0]
    for h in range(heads):
        hs = slice(h * LANE, (h + 1) * LANE)
        xh = xb[:, hs]
        z = jnp.dot(xh.astype(BF16), wg_ref[h], preferred_element_type=F32)
        r = _sigmoid(z[:, :LANE] + ba[:, hs])
        i = _sigmoid(z[:, LANE:] + bx[:, hs])
        log_a = (-LRU_C) * r * sp[:, hs]
        a = jnp.exp(log_a)
        u = jnp.sqrt(_neg_expm1_2x(log_a)) * (i * xh)
        a, u = _scan8(a, u, reverse)
        a_scr[:, hs] = a
        u_scr[:, hs] = u


def _lru_fwd_kernel(x_ref, xp_ref, xn_ref, mod_ref, g_ref, win_ref, cw_ref, cb_ref, wg_ref,
                    ba_ref, bx_ref, lam_ref, h0_ref,
                    gate_ref, xb_ref, hf_ref, fin_ref,
                    carry_ref, a_scr, u_scr, *, tt, n_t, width):
    t = pl.program_id(1)

    @pl.when(t == 0)
    def _():
        carry_ref[...] = h0_ref[0]

    ext = jnp.concatenate([xp_ref[0], x_ref[0], xn_ref[0]], axis=0)
    hn = _rms(ext, g_ref[...]) * (1.0 + mod_ref[0, 1:2, :]) + mod_ref[0, 0:1, :]
    proj = jnp.dot(hn.astype(BF16), win_ref[...], preferred_element_type=F32)
    gate_ref[0] = _gelu_tanh(proj[SUBLANE:SUBLANE + tt, :width])

    row = lax.broadcasted_iota(I32, (tt + 2 * SUBLANE, 1), 0)
    valid = ((row >= SUBLANE) | (t > 0)) & ((row < tt + SUBLANE) | (t < n_t - 1))
    xpre = jnp.where(valid, proj[:, width:], 0.0)
    left = CONV_W // 2
    xb = cb_ref[...]
    for k in range(CONV_W):
        s = SUBLANE - left + k
        xb = xb + cw_ref[k:k + 1, :] * xpre[s:s + tt, :]
    xb_ref[0] = xb

    sp = _softplus(-lam_ref[...])
    _lru_gates(xb, wg_ref, ba_ref[...], bx_ref[...], sp, a_scr, u_scr, reverse=False)

    def body(gi, carry):
        r0 = pl.multiple_of(gi * SUBLANE, SUBLANE)
        h8 = u_scr[pl.ds(r0, SUBLANE), :] + a_scr[pl.ds(r0, SUBLANE), :] * carry
        hf_ref[0, pl.ds(r0, SUBLANE), :] = h8
        return h8[SUBLANE - 1:SUBLANE, :]

    carry = lax.fori_loop(0, tt // SUBLANE, body, carry_ref[...])
    carry_ref[...] = carry
    fin_ref[0] = carry


def _lru_bwd_kernel(x_ref, gate_ref, xb_ref, hf_ref, mod_ref, wg_ref, ba_ref, bx_ref, lam_ref, h0_ref,
                    wout_ref, g2_ref, rt_ref,
                    xo_ref, hn_ref, lg_ref, fin_ref,
                    carry_ref, a_scr, u_scr, hb_scr, *, tt):
    t = pl.program_id(1)

    @pl.when(t == 0)
    def _():
        carry_ref[...] = h0_ref[0]

    xb = xb_ref[0]
    sp = _softplus(-lam_ref[...])
    _lru_gates(xb, wg_ref, ba_ref[...], bx_ref[...], sp, a_scr, u_scr, reverse=True)

    n_g = tt // SUBLANE

    def body(gi, carry):
        r0 = pl.multiple_of((n_g - 1 - gi) * SUBLANE, SUBLANE)
        h8 = u_scr[pl.ds(r0, SUBLANE), :] + a_scr[pl.ds(r0, SUBLANE), :] * carry
        hb_scr[pl.ds(r0, SUBLANE), :] = h8
        return h8[0:1, :]

    carry = lax.fori_loop(0, n_g, body, carry_ref[...])
    carry_ref[...] = carry
    fin_ref[0] = carry

    y = ((hf_ref[0] + hb_scr[...]) * gate_ref[0]).astype(BF16)
    y = jnp.dot(y, wout_ref[...], preferred_element_type=F32)
    xo = x_ref[0] + mod_ref[0, 2:3, :] * y
    xo_ref[0] = xo
    _moe_prenorm(xo, mod_ref, g2_ref, rt_ref, hn_ref, lg_ref)


def _moe_prenorm(xo, mod_ref, g2_ref, rt_ref, hn_ref, lg_ref):
    hn = _rms(xo, g2_ref[...]) * (1.0 + mod_ref[0, 4:5, :]) + mod_ref[0, 3:4, :]
    hn_ref[0] = hn.astype(BF16)
    lg_ref[...] = lax.dot_general(rt_ref[...], hn, (((1,), (1,)), ((), ())),
                                  preferred_element_type=F32, precision=HIGHEST)


def _lru_tile(T):
    return min(T, 256)


def _lru_layer(x, mod, h0, p, j, norm1_g, norm2_g, router_t):
    B, T, D = x.shape
    W = p["lru_lam"].shape[-1]
    H = W // LANE
    E = router_t.shape[0]
    tt = _lru_tile(T)
    n_t = T // tt
    bc = mod.shape[0]
    mod_map = (lambda b, t: (b, 0, 0)) if bc > 1 else (lambda b, t: (0, 0, 0))
    n8 = T // SUBLANE
    r8 = tt // SUBLANE

    def wg(d):
        return jnp.concatenate([p["lru_w_a"][j, d], p["lru_w_x"][j, d]], axis=-1).astype(BF16)

    def vec(name, d):
        return p[name][j, d].reshape(1, W)

    full = lambda shape: pl.BlockSpec(shape, lambda b, t: (0,) * len(shape))
    tile_w = lambda imap: pl.BlockSpec((1, tt, W), imap)
    fwd_map = lambda b, t: (b, t, 0)
    gate, xb, hf, fin_f = pl.pallas_call(
        functools.partial(_lru_fwd_kernel, tt=tt, n_t=n_t, width=W),
        grid=(B, n_t),
        in_specs=[pl.BlockSpec((1, tt, D), fwd_map),
                  pl.BlockSpec((1, SUBLANE, D), lambda b, t: (b, jnp.maximum(t * r8 - 1, 0), 0)),
                  pl.BlockSpec((1, SUBLANE, D), lambda b, t: (b, jnp.minimum((t + 1) * r8, n8 - 1), 0)),
                  pl.BlockSpec((1, 6, D), mod_map),
                  full((1, D)), full((D, 2 * W)), full((CONV_W, W)), full((1, W)),
                  full((H, LANE, 2 * LANE)), full((1, W)), full((1, W)), full((1, W)),
                  pl.BlockSpec((1, 1, W), lambda b, t: (b, 0, 0))],
        out_specs=[tile_w(fwd_map), tile_w(fwd_map), tile_w(fwd_map),
                   pl.BlockSpec((1, 1, W), lambda b, t: (b, 0, 0))],
        out_shape=[jax.ShapeDtypeStruct((B, T, W), F32)] * 3 + [jax.ShapeDtypeStruct((B, 1, W), F32)],
        scratch_shapes=[pltpu.VMEM((1, W), F32), pltpu.VMEM((tt, W), F32), pltpu.VMEM((tt, W), F32)],
        compiler_params=_cparams(("arbitrary", "arbitrary")),
        name="lru_fwd",
    )(x, x, x, mod, norm1_g.reshape(1, D), p["lru_w_in"][j].astype(BF16), p["lru_conv_w"][j],
      p["lru_conv_b"][j].reshape(1, W), wg(0), vec("lru_b_a", 0), vec("lru_b_x", 0), vec("lru_lam", 0),
      h0[:, 0:1])

    bwd_map = lambda b, t: (b, n_t - 1 - t, 0)
    xo, hn2, lg, fin_b = pl.pallas_call(
        functools.partial(_lru_bwd_kernel, tt=tt),
        grid=(B, n_t),
        in_specs=[pl.BlockSpec((1, tt, D), bwd_map), tile_w(bwd_map), tile_w(bwd_map), tile_w(bwd_map),
                  pl.BlockSpec((1, 6, D), mod_map),
                  full((H, LANE, 2 * LANE)), full((1, W)), full((1, W)), full((1, W)),
                  pl.BlockSpec((1, 1, W), lambda b, t: (b, 0, 0)),
                  full((W, D)), full((1, D)), full((E, D))],
        out_specs=[pl.BlockSpec((1, tt, D), bwd_map), pl.BlockSpec((1, tt, D), bwd_map),
                   pl.BlockSpec((E, tt), lambda b, t: (0, b * n_t + n_t - 1 - t)),
                   pl.BlockSpec((1, 1, W), lambda b, t: (b, 0, 0))],
        out_shape=[jax.ShapeDtypeStruct((B, T, D), F32), jax.ShapeDtypeStruct((B, T, D), BF16),
                   jax.ShapeDtypeStruct((E, B * T), F32), jax.ShapeDtypeStruct((B, 1, W), F32)],
        scratch_shapes=[pltpu.VMEM((1, W), F32), pltpu.VMEM((tt, W), F32), pltpu.VMEM((tt, W), F32),
                        pltpu.VMEM((tt, W), F32)],
        compiler_params=_cparams(("arbitrary", "arbitrary")),
        name="lru_bwd",
    )(x, gate, xb, hf, mod, wg(1), vec("lru_b_a", 1), vec("lru_b_x", 1), vec("lru_lam", 1), h0[:, 1:2],
      p["lru_w_out"][j].astype(BF16), norm2_g.reshape(1, D), router_t)
    return xo, hn2, lg, jnp.concatenate([fin_f, fin_b], axis=1)


def _sgu_kernel(x_ref, mod_ref, g_ref, win_ref, ng_ref, ws_ref, bs_ref, wout_ref, g2_ref, rt_ref,
                xo_ref, hn_ref, lg_ref, u_scr, v_scr, p_scr, *, tt, sgw):
    x = x_ref[0]
    hn = (_rms(x, g_ref[...]) * (1.0 + mod_ref[0, 1:2, :]) + mod_ref[0, 0:1, :]).astype(BF16)
    u_scr[...] = _gelu_tanh(jnp.dot(hn, win_ref[:, :sgw], preferred_element_type=F32))
    v = _gelu_tanh(jnp.dot(hn, win_ref[:, sgw:], preferred_element_type=F32))
    v_scr[...] = _rms(v, ng_ref[...]).astype(BF16)
    gd = sgw // SG_GROUPS
    for n in range(tt // CHUNK):
        rs = slice(n * CHUNK, (n + 1) * CHUNK)
        for g in range(SG_GROUPS):
            cs = slice(g * gd, (g + 1) * gd)
            sv = jnp.dot(ws_ref[g], v_scr[rs, cs], preferred_element_type=F32) + bs_ref[:, g:g + 1]
            p_scr[rs, cs] = (u_scr[rs, cs] * sv).astype(BF16)
    y = jnp.dot(p_scr[...], wout_ref[...], preferred_element_type=F32)
    xo = x + mod_ref[0, 2:3, :] * y
    xo_ref[0] = xo
    _moe_prenorm(xo, mod_ref, g2_ref, rt_ref, hn_ref, lg_ref)


def _sgu_layer(x, mod, p, j, norm1_g, norm2_g, router_t):
    B, T, D = x.shape
    sgw = p["sg_norm_g"].shape[-1]
    E = router_t.shape[0]
    tt = min(T, 256)
    n_t = T // tt
    bc = mod.shape[0]
    mod_map = (lambda b, t: (b, 0, 0)) if bc > 1 else (lambda b, t: (0, 0, 0))
    full = lambda shape: pl.BlockSpec(shape, lambda b, t: (0,) * len(shape))
    tile = pl.BlockSpec((1, tt, D), lambda b, t: (b, t, 0))
    return pl.pallas_call(
        functools.partial(_sgu_kernel, tt=tt, sgw=sgw),
        grid=(B, n_t),
        in_specs=[tile, pl.BlockSpec((1, 6, D), mod_map), full((1, D)), full((D, 2 * sgw)), full((1, sgw)),
                  full((SG_GROUPS, CHUNK, CHUNK)), full((CHUNK, SG_GROUPS)), full((sgw, D)),
                  full((1, D)), full((E, D))],
        out_specs=[tile, tile, pl.BlockSpec((E, tt), lambda b, t: (0, b * n_t + t))],
        out_shape=[jax.ShapeDtypeStruct((B, T, D), F32), jax.ShapeDtypeStruct((B, T, D), BF16),
                   jax.ShapeDtypeStruct((E, B * T), F32)],
        scratch_shapes=[pltpu.VMEM((tt, sgw), F32), pltpu.VMEM((tt, sgw), BF16), pltpu.VMEM((tt, sgw), BF16)],
        compiler_params=_cparams(("arbitrary", "arbitrary")),
        name="sgu",
    )(x, mod, norm1_g.reshape(1, D), p["sg_w_in"][j].astype(BF16), p["sg_norm_g"][j].reshape(1, sgw),
      p["sg_w_s"][j].astype(BF16), p["sg_b_s"][j].T, p["sg_w_out"][j].astype(BF16),
      norm2_g.reshape(1, D), router_t)


def _select_kernel(lg_ref, pos_ref, g_ref, off_ref, aff_scr, *, cap, tb):
    E, N = lg_ref.shape
    n_tile = N // tb
    lg = lg_ref[...]
    ex = jnp.exp(lg - jnp.max(lg, axis=0, keepdims=True))
    aff_scr[...] = ex / jnp.sum(ex, axis=0, keepdims=True)

    def search(it, cur):
        cand = cur | (1 << (30 - it))
        bits = pltpu.bitcast(aff_scr[...], I32)
        cnt = jnp.sum(jnp.where(bits >= cand, 1.0, 0.0), axis=1, keepdims=True)
        return jnp.where(cnt >= cap, cand, cur)

    thr = lax.fori_loop(0, 31, search, jnp.zeros((E, 1), I32))
    bits = pltpu.bitcast(aff_scr[...], I32)
    n_gt = jnp.sum(jnp.where(bits > thr, 1.0, 0.0), axis=1, keepdims=True)
    need = cap - n_gt

    tri = (lax.broadcasted_iota(I32, (tb, tb), 0) <= lax.broadcasted_iota(I32, (tb, tb), 1)).astype(BF16)
    lane = lax.broadcasted_iota(I32, off_ref.shape, 1)

    def chunk(c, carry):
        c_eq, c_pos = carry
        l0 = pl.multiple_of(c * tb, tb)
        aff = aff_scr[:, pl.ds(l0, tb)]
        b = pltpu.bitcast(aff, I32)
        eq = b == thr
        eqf = jnp.where(eq, 1.0, 0.0)
        rank = jnp.dot(eqf.astype(BF16), tri, preferred_element_type=F32) - eqf + c_eq
        sel = (b > thr) | (eq & (rank < need))
        self_ = jnp.where(sel, 1.0, 0.0)
        inc = jnp.dot(self_.astype(BF16), tri, preferred_element_type=F32)
        pos = inc - self_ + c_pos
        pos_ref[:, pl.ds(l0, tb)] = jnp.where(sel, pos, -1.0).astype(I32)
        g_ref[:, pl.ds(l0, tb)] = jnp.where(sel, aff, 0.0)
        off_ref[...] = jnp.where(lane == c, jnp.broadcast_to(c_pos, off_ref.shape).astype(I32), off_ref[...])
        return (c_eq + jnp.sum(eqf, axis=1, keepdims=True), c_pos + inc[:, tb - 1:tb])

    off_ref[...] = jnp.zeros(off_ref.shape, I32)
    zero = jnp.zeros((E, 1), F32)
    _, total = lax.fori_loop(0, n_tile, chunk, (zero, zero))
    off_ref[...] = jnp.where(lane == n_tile, jnp.broadcast_to(total, off_ref.shape).astype(I32), off_ref[...])


def _select(logits_t, cap, tb):
    E, N = logits_t.shape
    offw = -(-(N // tb + 1) // LANE) * LANE
    return pl.pallas_call(
        functools.partial(_select_kernel, cap=cap, tb=tb),
        out_shape=[jax.ShapeDtypeStruct((E, N), I32), jax.ShapeDtypeStruct((E, N), F32),
                   jax.ShapeDtypeStruct((E, offw), I32)],
        scratch_shapes=[pltpu.VMEM((E, N), F32)],
        compiler_params=pltpu.CompilerParams(vmem_limit_bytes=VMEM_LIMIT),
        name="moe_select",
    )(logits_t)


def _expert_kernel(off_ref, x_ref, pos_ref, g_ref, wg_ref, wu_ref, wd_ref, y_ref,
                   xs_scr, gs_scr, *, n_exp, n_gather, tb, sb, ws, offw):
    s = pl.program_id(0)
    d = pl.program_id(1)
    sub = x_ref.shape[0] // tb
    del n_exp

    @pl.when(d == 0)
    def _():
        xs_scr[...] = jnp.zeros(xs_scr.shape, xs_scr.dtype)
        gs_scr[...] = jnp.zeros(gs_scr.shape, gs_scr.dtype)

    @pl.when(d < n_gather)
    def _gather():
        for j in range(sub):
            i = d * sub + j
            o0 = off_ref[s * offw + i]
            cnt = off_ref[s * offw + i + 1] - o0
            base = (o0 // BF16_ROWS) * BF16_ROWS
            n_ch = jnp.where(cnt > 0, (o0 - base + cnt + ws - 1) // ws, 0)
            l0 = pl.multiple_of(i * tb, tb)
            prow = pos_ref[0, :, pl.ds(l0, tb)]
            grow = g_ref[0, :, pl.ds(l0, tb)]
            xt = x_ref[j * tb:(j + 1) * tb, :]

            def chunk(ch, _):
                r0 = pl.multiple_of(base + ch * ws, BF16_ROWS)
                hit = prow == (r0 + lax.broadcasted_iota(I32, (ws, tb), 0))
                rows = jnp.dot(jnp.where(hit, 1.0, 0.0).astype(BF16), xt, preferred_element_type=F32)
                xs_scr[pl.ds(r0, ws), :] += rows.astype(BF16)
                gcol = jnp.sum(jnp.where(hit, grow, 0.0), axis=1, keepdims=True)
                gs_scr[pl.ds(r0, ws), :] += jnp.broadcast_to(gcol, (ws, LANE))
                return 0

            lax.fori_loop(0, n_ch, chunk, 0)

    @pl.when(d >= n_gather)
    def _ffn():
        r0 = pl.multiple_of((d - n_gather) * sb, sb)
        xb = xs_scr[pl.ds(r0, sb), :]
        hg = jnp.dot(xb, wg_ref[0, 0], preferred_element_type=F32)
        hu = jnp.dot(xb, wu_ref[0, 0], preferred_element_type=F32)
        h = (hg * _sigmoid(hg) * hu).astype(BF16)
        o = jnp.dot(h, wd_ref[0, 0], preferred_element_type=F32)
        y_ref[...] = (o * gs_scr[pl.ds(r0, sb), 0:1]).astype(BF16)


def _experts(hn2, pos, g, off, w_gate, w_up, w_down, layer, cap, tb, sb, ws):
    N, D = hn2.shape
    E = pos.shape[0]
    F = w_gate.shape[-1]
    offw = off.shape[1]
    n_blk = cap // sb
    ns = min(N, GATHER_TOKENS)
    n_gather = N // ns
    assert ns % tb == 0 and cap % sb == 0 and N % ns == 0
    emap = lambda s, d, off: (s, 0, 0)
    wmap = lambda s, d, off: (layer, s, 0, 0)
    grid_spec = pltpu.PrefetchScalarGridSpec(
        num_scalar_prefetch=1,
        grid=(E, n_gather + n_blk),
        in_specs=[pl.BlockSpec((ns, D), lambda s, d, off: (jnp.minimum(d, n_gather - 1), 0)),
                  pl.BlockSpec((1, 1, N), emap), pl.BlockSpec((1, 1, N), emap),
                  pl.BlockSpec((1, 1, D, F), wmap), pl.BlockSpec((1, 1, D, F), wmap),
                  pl.BlockSpec((1, 1, F, D), wmap)],
        out_specs=pl.BlockSpec((sb, D), lambda s, d, off: (s * n_blk + jnp.maximum(d - n_gather, 0), 0)),
        scratch_shapes=[pltpu.VMEM((cap + ws, D), BF16), pltpu.VMEM((cap + ws, LANE), F32)],
    )
    return pl.pallas_call(
        functools.partial(_expert_kernel, n_exp=E, n_gather=n_gather, tb=tb, sb=sb, ws=ws, offw=offw),
        grid_spec=grid_spec,
        out_shape=jax.ShapeDtypeStruct((E * cap, D), BF16),
        compiler_params=_cparams(("arbitrary", "arbitrary")),
        name="moe_experts",
    )(off.reshape(-1), hn2, pos.reshape(E, 1, N), g.reshape(E, 1, N), w_gate, w_up, w_down)


def _window_base(off_ref, e, i, offw, cap, ws):
    o0 = off_ref[e * offw + i]
    return jnp.minimum((o0 // BF16_ROWS) * BF16_ROWS, cap - ws)


def _combine_kernel(off_ref, x_ref, pos_ref, mod_ref, y_hbm, xo_ref, win, extra, sem, xsem,
                    *, n_exp, n_tile, cap, ws, offw):
    i = pl.program_id(0)
    slot = i % 2
    tb = x_ref.shape[0]

    def window_copy(ti, sl, e):
        base = _window_base(off_ref, e, ti, offw, cap, ws)
        return pltpu.make_async_copy(y_hbm.at[pl.ds(e * cap + base, ws), :],
                                     win.at[sl, pl.ds(e * ws, ws), :], sem.at[sl, e])

    @pl.when(i == 0)
    def _():
        for e in range(n_exp):
            window_copy(0, 0, e).start()

    @pl.when(i + 1 < n_tile)
    def _():
        for e in range(n_exp):
            window_copy(i + 1, 1 - slot, e).start()

    post = jnp.transpose(pos_ref[...].astype(F32)).astype(I32)
    lane = lax.broadcasted_iota(I32, (tb, ws), 1)
    hits = []
    for e in range(n_exp):
        base = _window_base(off_ref, e, i, offw, cap, ws)
        hits.append(jnp.where(post[:, e:e + 1] == (base + lane), 1.0, 0.0).astype(BF16))
        window_copy(i, slot, e).wait()
    gate2 = mod_ref[0, 5:6, :]
    acc = jnp.dot(jnp.concatenate(hits, axis=1), win[slot], preferred_element_type=F32)
    xo_ref[...] = x_ref[...] + gate2 * acc

    for e in range(n_exp):
        end = off_ref[e * offw + i + 1]
        base = _window_base(off_ref, e, i, offw, cap, ws)
        n_more = jnp.maximum(end - (base + ws) + ws - 1, 0) // ws

        def more(k, _):
            b2 = jnp.minimum(base + (k + 1) * ws, cap - ws)
            cp = pltpu.make_async_copy(y_hbm.at[pl.ds(e * cap + b2, ws), :], extra, xsem)
            cp.start()
            cp.wait()
            pc = jnp.transpose(pos_ref[...].astype(F32)).astype(I32)[:, e:e + 1]
            h2 = (pc == (b2 + lane)) & (pc >= base + (k + 1) * ws)
            xo_ref[...] += gate2 * jnp.dot(jnp.where(h2, 1.0, 0.0).astype(BF16), extra[...],
                                           preferred_element_type=F32)
            return 0

        lax.fori_loop(0, n_more, more, 0)


def _combine(x2, pos, off, y2, mod, tokens_per_batch, cap, tb, ws):
    N, D = x2.shape
    E = pos.shape[0]
    offw = off.shape[1]
    n_tile = N // tb
    per_b = tokens_per_batch // tb
    bc = mod.shape[0]
    mod_map = (lambda i, off: (i // per_b, 0, 0)) if bc > 1 else (lambda i, off: (0, 0, 0))
    grid_spec = pltpu.PrefetchScalarGridSpec(
        num_scalar_prefetch=1,
        grid=(n_tile,),
        in_specs=[pl.BlockSpec((tb, D), lambda i, off: (i, 0)),
                  pl.BlockSpec((E, tb), lambda i, off: (0, i)),
                  pl.BlockSpec((1, 6, D), mod_map),
                  pl.BlockSpec(memory_space=pl.ANY)],
        out_specs=pl.BlockSpec((tb, D), lambda i, off: (i, 0)),
        scratch_shapes=[pltpu.VMEM((2, E * ws, D), BF16), pltpu.VMEM((ws, D), BF16),
                        pltpu.SemaphoreType.DMA((2, E)), pltpu.SemaphoreType.DMA(())],
    )
    return pl.pallas_call(
        functools.partial(_combine_kernel, n_exp=E, n_tile=n_tile, cap=cap, ws=ws, offw=offw),
        grid_spec=grid_spec,
        out_shape=jax.ShapeDtypeStruct((N, D), F32),
        compiler_params=_cparams(("arbitrary",)),
        name="moe_combine",
    )(off.reshape(-1), x2, pos, mod, y2)


def _moe(x, hn2, logits_t, mod, p, layer):
    B, T, D = x.shape
    N = B * T
    E = logits_t.shape[0]
    cap = EC_FACTOR * N // E
    tb = min(TOKEN_TILE, T)
    sb = min(SLOT_BLOCK, cap)
    ws = min(SLOT_WINDOW, cap)
    pos, g, off = _select(logits_t, cap, tb)
    y2 = _experts(hn2.reshape(N, D), pos, g, off, p["moe_w_gate"], p["moe_w_up"], p["moe_w_down"],
                  layer, cap, tb, sb, ws)
    xo = _combine(x.reshape(N, D), pos, off, y2, mod, T, cap, tb, ws)
    return xo.reshape(B, T, D)


def _final_kernel(x_ref, g_ref, o_ref):
    o_ref[...] = _rms(x_ref[...], g_ref[...])


def _final_norm(x, g):
    B, T, D = x.shape
    N = B * T
    tn = min(N, 1024)
    return pl.pallas_call(
        _final_kernel,
        grid=(N // tn,),
        in_specs=[pl.BlockSpec((tn, D), lambda i: (i, 0)), pl.BlockSpec((1, D), lambda i: (0, 0))],
        out_specs=pl.BlockSpec((tn, D), lambda i: (i, 0)),
        out_shape=jax.ShapeDtypeStruct((N, D), F32),
        compiler_params=_cparams(("arbitrary",)),
        name="final_norm",
    )(x.reshape(N, D), g.reshape(1, D)).reshape(B, T, D)


def _grid_pos_embed(n_tokens, d_model):
    rows = n_tokens // GRID_W
    row = jnp.repeat(jnp.arange(rows, dtype=F32), GRID_W)
    col = jnp.tile(jnp.arange(GRID_W, dtype=F32), rows)
    q = d_model // 4
    freq = jnp.exp(-math.log(POS_BASE) * jnp.arange(q, dtype=F32) / q)
    ang_r = row[:, None] * freq
    ang_c = col[:, None] * freq
    return jnp.concatenate([jnp.sin(ang_r), jnp.cos(ang_r), jnp.sin(ang_c), jnp.cos(ang_c)], axis=-1)


def _trunk(x, mods, h0, p):
    depth = p["norm1_g"].shape[0]
    finals = []
    for l in range(depth):
        mod = mods[l]
        router_t = p["moe_router"][l].T
        j = l // 2
        if l % 2 == 0:
            x, hn2, lg, fin = _lru_layer(x, mod, h0[:, j], p, j, p["norm1_g"][l], p["norm2_g"][l], router_t)
            finals.append(fin)
        else:
            x, hn2, lg = _sgu_layer(x, mod, p, j, p["norm1_g"][l], p["norm2_g"][l], router_t)
        x = _moe(x, hn2, lg, mod, p, l)
    return _final_norm(x, p["final_norm_g"]), jnp.stack(finals, axis=1)


def kernel(x_prompt, x_sample, state_lru, c, c_ctx, norm1_g, norm2_g, w_mod, b_mod,
           lru_w_in, lru_conv_w, lru_conv_b, lru_w_a, lru_b_a, lru_w_x, lru_b_x, lru_lam, lru_w_out,
           sg_w_in, sg_norm_g, sg_w_s, sg_b_s, sg_w_out,
           moe_router, moe_w_gate, moe_w_up, moe_w_down, final_norm_g):
    p = dict(norm1_g=norm1_g, norm2_g=norm2_g, lru_w_in=lru_w_in, lru_conv_w=lru_conv_w,
             lru_conv_b=lru_conv_b, lru_w_a=lru_w_a, lru_b_a=lru_b_a, lru_w_x=lru_w_x, lru_b_x=lru_b_x,
             lru_lam=lru_lam, lru_w_out=lru_w_out, sg_w_in=sg_w_in, sg_norm_g=sg_norm_g, sg_w_s=sg_w_s,
             sg_b_s=sg_b_s, sg_w_out=sg_w_out, moe_router=moe_router, moe_w_gate=moe_w_gate.astype(BF16),
             moe_w_up=moe_w_up.astype(BF16), moe_w_down=moe_w_down.astype(BF16), final_norm_g=final_norm_g)
    L, D, _ = w_mod.shape
    bs = c.shape[0]
    n_lru, _, W = lru_lam.shape

    rows = -(-(1 + bs) // SUBLANE) * SUBLANE
    cond = jnp.zeros((rows, D), F32).at[0].set(c_ctx).at[1:1 + bs].set(c)
    mods = _modulation(cond, w_mod, b_mod).reshape(L, rows, 6, D)

    h0_ctx = jnp.zeros((x_prompt.shape[0], n_lru, 2, W), F32)
    y_prompt, new_state = _trunk(x_prompt, mods[:, 0:1], h0_ctx, p)

    xs = _add_pos(x_sample, _grid_pos_embed(x_sample.shape[1], D))
    y_sample, _ = _trunk(xs, mods[:, 1:1 + bs], state_lru, p)
    return (y_prompt, y_sample, new_state)
```

```python
import functools
import math

import jax
import jax.numpy as jnp
from jax import lax
from jax.experimental import pallas as pl
from jax.experimental.pallas import tpu as pltpu

F32 = jnp.float32
BF16 = jnp.bfloat16
I32 = jnp.int32
HIGHEST = lax.Precision.HIGHEST

RMS_EPS = 1e-6
LRU_C = 8.0
CONV_W = 4
CHUNK = 128
SG_GROUPS = 8
GRID_W = 64
POS_BASE = 10000.0
EC_FACTOR = 2
LANE = 128
SUBLANE = 8
BF16_ROWS = 16
VMEM_LIMIT = 56 * 1024 * 1024

TOKEN_TILE = 256
SLOT_BLOCK = 1024
GATHER_TOKENS = 2048
LRU_ROWS = 512
STAGING_BYTES = 22 * 1024 * 1024
COMBINE_WINDOW = 128
DIGIT_BITS = 6
NO_SLOT = 1 << 14
SLOT_WINDOW = 64


def _cparams(sem):
    return pltpu.CompilerParams(dimension_semantics=sem, vmem_limit_bytes=VMEM_LIMIT)


def _rms(x, g):
    return x * lax.rsqrt(jnp.mean(x * x, axis=-1, keepdims=True) + RMS_EPS) * g


def _gelu_tanh(x):
    c = math.sqrt(2.0 / math.pi)
    return x * (0.5 * (1.0 + jnp.tanh(c * (x + 0.044715 * (x * x * x)))))


def _sigmoid(x):
    return 1.0 / (1.0 + jnp.exp(-x))


def _log1p(e):
    w = 1.0 + e
    return jnp.where(w == 1.0, e, e * jnp.log(w) / jnp.where(w == 1.0, 1.0, w - 1.0))


def _softplus(x):
    return jnp.maximum(x, 0.0) + _log1p(jnp.exp(-jnp.abs(x)))


def _neg_expm1_2x(x):
    t = jnp.tanh(x)
    return (-2.0 * t) / (1.0 - t)


def _mod_kernel(c_ref, w_ref, b_ref, o_ref):
    c = c_ref[...]
    sc = c * _sigmoid(c)
    o_ref[0] = jnp.dot(sc, w_ref[0], preferred_element_type=F32, precision=HIGHEST) + b_ref[0]


def _modulation(cond, w_mod, b_mod):
    L, D, D6 = w_mod.shape
    R = cond.shape[0]
    tn = D6 // 4
    return pl.pallas_call(
        _mod_kernel,
        grid=(L, D6 // tn),
        in_specs=[pl.BlockSpec((R, D), lambda l, n: (0, 0)),
                  pl.BlockSpec((1, D, tn), lambda l, n: (l, 0, n)),
                  pl.BlockSpec((1, 1, tn), lambda l, n: (l, 0, n))],
        out_specs=pl.BlockSpec((1, R, tn), lambda l, n: (l, 0, n)),
        out_shape=jax.ShapeDtypeStruct((L, R, D6), F32),
        compiler_params=_cparams(("arbitrary", "arbitrary")),
        name="modulation",
    )(cond, w_mod, b_mod.reshape(L, 1, D6))


def _add_kernel(x_ref, p_ref, o_ref):
    o_ref[0] = x_ref[0] + p_ref[...]


def _add_pos(x, pe):
    B, T, D = x.shape
    tt = min(T, 512)
    return pl.pallas_call(
        _add_kernel,
        grid=(T // tt, B),
        in_specs=[pl.BlockSpec((1, tt, D), lambda t, b: (b, t, 0)),
                  pl.BlockSpec((tt, D), lambda t, b: (t, 0))],
        out_specs=pl.BlockSpec((1, tt, D), lambda t, b: (b, t, 0)),
        out_shape=jax.ShapeDtypeStruct((B, T, D), F32),
        compiler_params=_cparams(("arbitrary", "arbitrary")),
        name="add_pos",
    )(x, pe)


def _lru_gates(xb, wg_ref, ba, bx, sp, a_scr, u_scr):
    heads = wg_ref.shape[0]
    for h in range(heads):
        hs = slice(h * LANE, (h + 1) * LANE)
        xh = xb[:, hs]
        z = jnp.dot(xh.astype(BF16), wg_ref[h], preferred_element_type=F32)
        r = _sigmoid(z[:, :LANE] + ba[:, hs])
        i = _sigmoid(z[:, LANE:] + bx[:, hs])
        log_a = (-LRU_C) * r * sp[:, hs]
        a = jnp.exp(log_a)
        u = jnp.sqrt(_neg_expm1_2x(log_a)) * (i * xh)
        a_scr[:, hs] = a
        u_scr[:, hs] = u


def _lru_fwd_kernel(x_ref, xn_ref, mod_ref, g_ref, win_ref, cw_ref, cb_ref, wg_ref,
                    ba_ref, bx_ref, lam_ref, h0_ref,
                    gate_ref, xb_ref, hf_ref, fin_ref,
                    carry_ref, prev_scr, a_scr, u_scr, *, tt, n_t, width):
    t = pl.program_id(0)
    nb = x_ref.shape[0]
    rows = tt * nb

    @pl.when(t == 0)
    def _():
        carry_ref[...] = h0_ref[...]
        prev_scr[...] = jnp.zeros(prev_scr.shape, F32)

    ext = jnp.concatenate([jnp.swapaxes(x_ref[...], 0, 1), xn_ref[:, 0, :][None]], axis=0)
    hn = _rms(ext, g_ref[...]) * (1.0 + mod_ref[:, 1, :][None]) + mod_ref[:, 0, :][None]
    hn = hn.reshape(rows + nb, hn.shape[-1]).astype(BF16)
    proj = jnp.dot(hn, win_ref[...], preferred_element_type=F32)
    gate_ref[0] = _gelu_tanh(proj[:rows, :width])

    ahead = jnp.where(t < n_t - 1, proj[rows:, width:], 0.0)
    span = jnp.concatenate([prev_scr[...], proj[:rows, width:], ahead], axis=0)
    prev_scr[...] = proj[rows - 2 * nb:rows, width:]
    xb = cb_ref[...]
    for k in range(CONV_W):
        xb = xb + cw_ref[k:k + 1, :] * span[k * nb:k * nb + rows, :]
    xb_ref[0] = xb

    sp = _softplus(-lam_ref[...])
    _lru_gates(xb, wg_ref, ba_ref[...], bx_ref[...], sp, a_scr, u_scr)

    def body(ti, h):
        r0 = pl.multiple_of(ti * nb, nb)
        h = u_scr[pl.ds(r0, nb), :] + a_scr[pl.ds(r0, nb), :] * h
        hf_ref[0, pl.ds(r0, nb), :] = h
        return h

    h = lax.fori_loop(0, tt, body, carry_ref[...])
    carry_ref[...] = h
    fin_ref[...] = h


def _lru_bwd_kernel(x_ref, gate_ref, xb_ref, hf_ref, mod_ref, wg_ref, ba_ref, bx_ref, lam_ref, h0_ref,
                    wout_ref, g2_ref, r_ref,
                    xo_ref, hn_ref, lg_ref, fin_ref,
                    carry_ref, a_scr, u_scr, hb_scr, *, tt):
    t = pl.program_id(0)
    nb = x_ref.shape[0]

    @pl.when(t == 0)
    def _():
        carry_ref[...] = h0_ref[...]

    sp = _softplus(-lam_ref[...])
    _lru_gates(xb_ref[0], wg_ref, ba_ref[...], bx_ref[...], sp, a_scr, u_scr)

    def body(ti, h):
        r0 = pl.multiple_of((tt - 1 - ti) * nb, nb)
        h = u_scr[pl.ds(r0, nb), :] + a_scr[pl.ds(r0, nb), :] * h
        hb_scr[pl.ds(r0, nb), :] = h
        return h

    h = lax.fori_loop(0, tt, body, carry_ref[...])
    carry_ref[...] = h
    fin_ref[...] = h

    y = ((hf_ref[0] + hb_scr[...]) * gate_ref[0]).astype(BF16)
    y = jnp.dot(y, wout_ref[...], preferred_element_type=F32)
    y = jnp.swapaxes(y.reshape(tt, nb, y.shape[-1]), 0, 1)
    xo = x_ref[...] + mod_ref[:, 2:3, :] * y
    xo_ref[...] = xo
    hn = _rms(xo, g2_ref[...]) * (1.0 + mod_ref[:, 4:5, :]) + mod_ref[:, 3:4, :]
    hn_ref[...] = hn.astype(BF16)
    lg = _router_logits(hn.reshape(nb * tt, hn.shape[-1]), r_ref)
    lg_ref[...] = lg.reshape(nb, tt, lg.shape[-1])


def _router_logits(hn, r_ref):
    h_hi = hn.astype(BF16)
    h_lo = (hn - h_hi.astype(F32)).astype(BF16)
    return (jnp.dot(h_hi, r_ref[0], preferred_element_type=F32)
            + jnp.dot(h_hi, r_ref[1], preferred_element_type=F32)
            + jnp.dot(h_lo, r_ref[0], preferred_element_type=F32))


def _split_router(router):
    hi = router.astype(BF16)
    lo = (router - hi.astype(F32)).astype(BF16)
    return jnp.stack([hi, lo])


def _moe_prenorm(xo, mod_ref, g2_ref, r_ref, hn_ref, lg_ref):
    hn = _rms(xo, g2_ref[...]) * (1.0 + mod_ref[0, 4:5, :]) + mod_ref[0, 3:4, :]
    hn_ref[0] = hn.astype(BF16)
    lg_ref[0] = _router_logits(hn, r_ref)


def _lru_layer(x, mod, h0, p, j, norm1_g, norm2_g, router2):
    B, T, D = x.shape
    W = p["lru_lam"].shape[-1]
    H = W // LANE
    E = router2.shape[-1]
    tt = min(T, max(SUBLANE, LRU_ROWS // B))
    n_t = T // tt
    rows = tt * B
    bc = mod.shape[0]
    assert B % SUBLANE == 0 and T % tt == 0 and tt % SUBLANE == 0 and bc in (1, B)
    n8 = T // SUBLANE
    r8 = tt // SUBLANE

    def wg(d):
        return jnp.concatenate([p["lru_w_a"][j, d], p["lru_w_x"][j, d]], axis=-1).astype(BF16)

    def vec(name, d):
        return p[name][j, d].reshape(1, W)

    full = lambda shape: pl.BlockSpec(shape, lambda t: (0,) * len(shape), pipeline_mode=pl.Buffered(1))
    fwd_x = lambda t: (0, t, 0)
    fwd_w = lambda t: (t, 0, 0)
    tile_w = lambda imap: pl.BlockSpec((1, rows, W), imap)
    inner = jax.ShapeDtypeStruct((n_t, rows, W), F32)
    gate, xb, hf, fin_f = pl.pallas_call(
        functools.partial(_lru_fwd_kernel, tt=tt, n_t=n_t, width=W),
        grid=(n_t,),
        in_specs=[pl.BlockSpec((B, tt, D), fwd_x),
                  pl.BlockSpec((B, SUBLANE, D), lambda t: (0, jnp.minimum((t + 1) * r8, n8 - 1), 0)),
                  full((bc, 6, D)),
                  full((1, D)), full((D, 2 * W)), full((CONV_W, W)), full((1, W)),
                  full((H, LANE, 2 * LANE)), full((1, W)), full((1, W)), full((1, W)),
                  full((B, W))],
        out_specs=[tile_w(fwd_w), tile_w(fwd_w), tile_w(fwd_w), pl.BlockSpec((B, W), lambda t: (0, 0))],
        out_shape=[inner] * 3 + [jax.ShapeDtypeStruct((B, W), F32)],
        scratch_shapes=[pltpu.VMEM((B, W), F32), pltpu.VMEM((2 * B, W), F32),
                        pltpu.VMEM((rows, W), F32), pltpu.VMEM((rows, W), F32)],
        compiler_params=_cparams(("arbitrary",)),
        name="lru_fwd",
    )(x, x, mod, norm1_g.reshape(1, D), p["lru_w_in"][j].astype(BF16), p["lru_conv_w"][j],
      p["lru_conv_b"][j].reshape(1, W), wg(0), vec("lru_b_a", 0), vec("lru_b_x", 0), vec("lru_lam", 0),
      h0[:, 0])

    bwd_x = lambda t: (0, n_t - 1 - t, 0)
    bwd_w = lambda t: (n_t - 1 - t, 0, 0)
    xo, hn2, lg, fin_b = pl.pallas_call(
        functools.partial(_lru_bwd_kernel, tt=tt),
        grid=(n_t,),
        in_specs=[pl.BlockSpec((B, tt, D), bwd_x), tile_w(bwd_w), tile_w(bwd_w), tile_w(bwd_w),
                  full((bc, 6, D)),
                  full((H, LANE, 2 * LANE)), full((1, W)), full((1, W)), full((1, W)),
                  full((B, W)),
                  full((W, D)), full((1, D)), full((2, D, E))],
        out_specs=[pl.BlockSpec((B, tt, D), bwd_x), pl.BlockSpec((B, tt, D), bwd_x),
                   pl.BlockSpec((B, tt, E), bwd_x), pl.BlockSpec((B, W), lambda t: (0, 0))],
        out_shape=[jax.ShapeDtypeStruct((B, T, D), F32), jax.ShapeDtypeStruct((B, T, D), BF16),
                   jax.ShapeDtypeStruct((B, T, E), F32), jax.ShapeDtypeStruct((B, W), F32)],
        scratch_shapes=[pltpu.VMEM((B, W), F32), pltpu.VMEM((rows, W), F32), pltpu.VMEM((rows, W), F32),
                        pltpu.VMEM((rows, W), F32)],
        compiler_params=_cparams(("arbitrary",)),
        name="lru_bwd",
    )(x, gate, xb, hf, mod, wg(1), vec("lru_b_a", 1), vec("lru_b_x", 1), vec("lru_lam", 1), h0[:, 1],
      p["lru_w_out"][j].astype(BF16), norm2_g.reshape(1, D), router2)
    return xo, hn2, lg.reshape(B * T, E).T, jnp.stack([fin_f, fin_b], axis=1)


def _sgu_kernel(x_ref, mod_ref, g_ref, win_ref, ng_ref, ws_ref, bs_ref, wout_ref, g2_ref, rt_ref,
                xo_ref, hn_ref, lg_ref, u_scr, v_scr, p_scr, *, tt, sgw):
    x = x_ref[0]
    hn = (_rms(x, g_ref[...]) * (1.0 + mod_ref[0, 1:2, :]) + mod_ref[0, 0:1, :]).astype(BF16)
    u_scr[...] = _gelu_tanh(jnp.dot(hn, win_ref[:, :sgw], preferred_element_type=F32))
    v = _gelu_tanh(jnp.dot(hn, win_ref[:, sgw:], preferred_element_type=F32))
    v_scr[...] = _rms(v, ng_ref[...]).astype(BF16)
    gd = sgw // SG_GROUPS
    for n in range(tt // CHUNK):
        rs = slice(n * CHUNK, (n + 1) * CHUNK)
        for g in range(SG_GROUPS):
            cs = slice(g * gd, (g + 1) * gd)
            sv = jnp.dot(ws_ref[g], v_scr[rs, cs], preferred_element_type=F32) + bs_ref[:, g:g + 1]
            p_scr[rs, cs] = (u_scr[rs, cs] * sv).astype(BF16)
    y = jnp.dot(p_scr[...], wout_ref[...], preferred_element_type=F32)
    xo = x + mod_ref[0, 2:3, :] * y
    xo_ref[0] = xo
    _moe_prenorm(xo, mod_ref, g2_ref, rt_ref, hn_ref, lg_ref)


def _sgu_layer(x, mod, p, j, norm1_g, norm2_g, router2):
    B, T, D = x.shape
    sgw = p["sg_norm_g"].shape[-1]
    E = router2.shape[-1]
    tt = min(T, 512)
    n_t = T // tt
    bc = mod.shape[0]
    mod_map = (lambda b, t: (b, 0, 0)) if bc > 1 else (lambda b, t: (0, 0, 0))
    full = lambda shape: pl.BlockSpec(shape, lambda b, t: (0,) * len(shape), pipeline_mode=pl.Buffered(1))
    tile = pl.BlockSpec((1, tt, D), lambda b, t: (b, t, 0))
    xo, hn2, lg = pl.pallas_call(
        functools.partial(_sgu_kernel, tt=tt, sgw=sgw),
        grid=(B, n_t),
        in_specs=[tile, pl.BlockSpec((1, 6, D), mod_map), full((1, D)), full((D, 2 * sgw)), full((1, sgw)),
                  full((SG_GROUPS, CHUNK, CHUNK)), full((CHUNK, SG_GROUPS)), full((sgw, D)),
                  full((1, D)), full((2, D, E))],
        out_specs=[tile, tile, pl.BlockSpec((1, tt, E), lambda b, t: (b, t, 0))],
        out_shape=[jax.ShapeDtypeStruct((B, T, D), F32), jax.ShapeDtypeStruct((B, T, D), BF16),
                   jax.ShapeDtypeStruct((B, T, E), F32)],
        scratch_shapes=[pltpu.VMEM((tt, sgw), F32), pltpu.VMEM((tt, sgw), BF16), pltpu.VMEM((tt, sgw), BF16)],
        compiler_params=_cparams(("arbitrary", "arbitrary")),
        name="sgu",
    )(x, mod, norm1_g.reshape(1, D), p["sg_w_in"][j].astype(BF16), p["sg_norm_g"][j].reshape(1, sgw),
      p["sg_w_s"][j].astype(BF16), p["sg_b_s"][j].T, p["sg_w_out"][j].astype(BF16),
      norm2_g.reshape(1, D), router2)
    return xo, hn2, lg.reshape(B * T, E).T


def _select_kernel(lg_ref, pos_ref, g_ref, off_ref, aff_scr, *, cap, tb):
    E, N = lg_ref.shape
    n_tile = N // tb
    lg = lg_ref[...]
    ex = jnp.exp(lg - jnp.max(lg, axis=0, keepdims=True))
    aff_scr[...] = ex / jnp.sum(ex, axis=0, keepdims=True)

    def search(it, cur):
        cand = cur | (1 << (30 - it))
        bits = pltpu.bitcast(aff_scr[...], I32)
        cnt = jnp.sum(jnp.where(bits >= cand, 1.0, 0.0), axis=1, keepdims=True)
        return jnp.where(cnt >= cap, cand, cur)

    thr = lax.fori_loop(0, 31, search, jnp.zeros((E, 1), I32))
    bits = pltpu.bitcast(aff_scr[...], I32)
    n_gt = jnp.sum(jnp.where(bits > thr, 1.0, 0.0), axis=1, keepdims=True)
    need = cap - n_gt

    tri = (lax.broadcasted_iota(I32, (tb, tb), 0) <= lax.broadcasted_iota(I32, (tb, tb), 1)).astype(BF16)
    lane = lax.broadcasted_iota(I32, off_ref.shape, 1)

    def chunk(c, carry):
        c_eq, c_pos = carry
        l0 = pl.multiple_of(c * tb, tb)
        aff = aff_scr[:, pl.ds(l0, tb)]
        b = pltpu.bitcast(aff, I32)
        eq = b == thr
        eqf = jnp.where(eq, 1.0, 0.0)
        rank = jnp.dot(eqf.astype(BF16), tri, preferred_element_type=F32) - eqf + c_eq
        sel = (b > thr) | (eq & (rank < need))
        self_ = jnp.where(sel, 1.0, 0.0)
        inc = jnp.dot(self_.astype(BF16), tri, preferred_element_type=F32)
        pos = inc - self_ + c_pos
        pos_ref[:, pl.ds(l0, tb)] = jnp.where(sel, pos, -1.0).astype(I32)
        g_ref[:, pl.ds(l0, tb)] = jnp.where(sel, aff, 0.0)
        off_ref[...] = jnp.where(lane == c, jnp.broadcast_to(c_pos, off_ref.shape).astype(I32), off_ref[...])
        return (c_eq + jnp.sum(eqf, axis=1, keepdims=True), c_pos + inc[:, tb - 1:tb])

    off_ref[...] = jnp.zeros(off_ref.shape, I32)
    zero = jnp.zeros((E, 1), F32)
    _, total = lax.fori_loop(0, n_tile, chunk, (zero, zero))
    off_ref[...] = jnp.where(lane == n_tile, jnp.broadcast_to(total, off_ref.shape).astype(I32), off_ref[...])


def _select(logits_t, cap, tb):
    E, N = logits_t.shape
    offw = -(-(N // tb + 1) // LANE) * LANE
    return pl.pallas_call(
        functools.partial(_select_kernel, cap=cap, tb=tb),
        out_shape=[jax.ShapeDtypeStruct((E, N), I32), jax.ShapeDtypeStruct((E, N), F32),
                   jax.ShapeDtypeStruct((E, offw), I32)],
        scratch_shapes=[pltpu.VMEM((E, N), F32)],
        compiler_params=pltpu.CompilerParams(vmem_limit_bytes=VMEM_LIMIT),
        name="moe_select",
    )(logits_t)


def _expert_kernel(off_ref, x_ref, pos_ref, g_ref, wg_ref, wu_ref, wd_ref, y_ref,
                   xs_scr, gs_scr, *, n_gather, n_blk, tb, sb, ws, offw):
    s = pl.program_id(0)
    d = pl.program_id(1)
    sub = x_ref.shape[1] // tb
    grp = xs_scr.shape[0]
    row_iota = lax.broadcasted_iota(I32, (ws, tb), 0)

    @pl.when(d == 0)
    def _():
        xs_scr[...] = jnp.zeros(xs_scr.shape, xs_scr.dtype)
        gs_scr[...] = jnp.zeros(gs_scr.shape, gs_scr.dtype)

    def add_rows(k, r0, hit, rows, grow):
        xs_scr[k, pl.ds(r0, ws), :] += rows.astype(BF16)
        gcol = jnp.sum(jnp.where(hit, grow, 0.0), axis=1, keepdims=True)
        gs_scr[k, pl.ds(r0, ws), :] += jnp.broadcast_to(gcol, (ws, LANE))

    @pl.when(d < n_gather)
    def _gather():
        for j in range(sub):
            i = d * sub + j
            xt = x_ref[0, j * tb:(j + 1) * tb, :]
            lanes = slice(j * tb, (j + 1) * tb)
            bases, hits = [], []
            for k in range(grp):
                e = s * grp + k
                o0 = off_ref[e * offw + i]
                base = pl.multiple_of((o0 // BF16_ROWS) * BF16_ROWS, BF16_ROWS)
                bases.append(base)
                hits.append(pos_ref[k, 0, :, lanes] == (base + row_iota))
            stacked = jnp.where(jnp.concatenate(hits, axis=0), 1.0, 0.0).astype(BF16)
            rows = jnp.dot(stacked, xt, preferred_element_type=F32)
            for k in range(grp):
                add_rows(k, bases[k], hits[k], rows[k * ws:(k + 1) * ws, :], g_ref[k, 0, :, lanes])
            for k in range(grp):
                e = s * grp + k
                end = off_ref[e * offw + i + 1]
                n_more = jnp.maximum(end - (bases[k] + ws) + ws - 1, 0) // ws

                def more(ch, _):
                    r0 = pl.multiple_of(bases[k] + (ch + 1) * ws, BF16_ROWS)
                    hit = pos_ref[k, 0, :, lanes] == (r0 + row_iota)
                    extra = jnp.dot(jnp.where(hit, 1.0, 0.0).astype(BF16), xt, preferred_element_type=F32)
                    add_rows(k, r0, hit, extra, g_ref[k, 0, :, lanes])
                    return 0

                lax.fori_loop(0, n_more, more, 0)

    @pl.when(d >= n_gather)
    def _ffn():
        step = d - n_gather
        k = step // n_blk
        r0 = pl.multiple_of((step - k * n_blk) * sb, sb)
        xb = xs_scr[k, pl.ds(r0, sb), :]
        hg = jnp.dot(xb, wg_ref[0, 0], preferred_element_type=F32)
        hu = jnp.dot(xb, wu_ref[0, 0], preferred_element_type=F32)
        h = (hg * _sigmoid(hg) * hu).astype(BF16)
        o = jnp.dot(h, wd_ref[0, 0], preferred_element_type=F32)
        y_ref[...] = (o * gs_scr[k, pl.ds(r0, sb), 0:1]).astype(BF16)


def _experts(hn2, pos, g, off, w_gate, w_up, w_down, layer, cap, tb, sb, ws, grp):
    N, D = hn2.shape
    E = pos.shape[0]
    F = w_gate.shape[-1]
    offw = off.shape[1]
    n_blk = cap // sb
    ns = min(N, GATHER_TOKENS)
    n_gather = N // ns
    assert ns % tb == 0 and cap % sb == 0 and N % ns == 0 and E % grp == 0
    tok = lambda s, d, off: jnp.minimum(d, n_gather - 1)
    ffn = lambda d: jnp.maximum(d - n_gather, 0)
    emap = lambda s, d, off: (s, tok(s, d, off), 0, 0)
    wmap = lambda s, d, off: (layer, s * grp + ffn(d) // n_blk, 0, 0)
    grid_spec = pltpu.PrefetchScalarGridSpec(
        num_scalar_prefetch=1,
        grid=(E // grp, n_gather + grp * n_blk),
        in_specs=[pl.BlockSpec((1, ns, D), lambda s, d, off: (tok(s, d, off), 0, 0)),
                  pl.BlockSpec((grp, 1, 1, ns), emap), pl.BlockSpec((grp, 1, 1, ns), emap),
                  pl.BlockSpec((1, 1, D, F), wmap), pl.BlockSpec((1, 1, D, F), wmap),
                  pl.BlockSpec((1, 1, F, D), wmap)],
        out_specs=pl.BlockSpec((sb, D), lambda s, d, off: (s * grp * n_blk + ffn(d), 0)),
        scratch_shapes=[pltpu.VMEM((grp, cap + ws, D), BF16), pltpu.VMEM((grp, cap + ws, LANE), F32)],
    )
    return pl.pallas_call(
        functools.partial(_expert_kernel, n_gather=n_gather, n_blk=n_blk, tb=tb, sb=sb, ws=ws, offw=offw),
        grid_spec=grid_spec,
        out_shape=jax.ShapeDtypeStruct((E * cap, D), BF16),
        compiler_params=_cparams(("arbitrary", "arbitrary")),
        name="moe_experts",
    )(off.reshape(-1), hn2.reshape(n_gather, ns, D), pos.reshape(E, n_gather, 1, ns),
      g.reshape(E, n_gather, 1, ns),
      w_gate, w_up, w_down)


def _window_base(off_ref, e, i, offw, cap, ws):
    o0 = off_ref[e * offw + i]
    return jnp.minimum((o0 // BF16_ROWS) * BF16_ROWS, cap - ws)


def _combine_kernel(off_ref, x_ref, pos_ref, mod_ref, y_hbm, xo_ref, win, extra, sem, xsem,
                    *, n_exp, n_tile, cap, ws, offw):
    i = pl.program_id(0)
    slot = i % 2
    tb = x_ref.shape[0]

    def window_copy(ti, sl, e):
        base = _window_base(off_ref, e, ti, offw, cap, ws)
        return pltpu.make_async_copy(y_hbm.at[pl.ds(e * cap + base, ws), :],
                                     win.at[sl, pl.ds(e * ws, ws), :], sem.at[sl, e])

    @pl.when(i == 0)
    def _():
        for e in range(n_exp):
            window_copy(0, 0, e).start()

    @pl.when(i + 1 < n_tile)
    def _():
        for e in range(n_exp):
            window_copy(i + 1, 1 - slot, e).start()

    kw = n_exp * ws
    shift = ws.bit_length() - 1
    post = jnp.transpose(pos_ref[...].astype(F32)).astype(I32)
    digits = jnp.where(post < 0, NO_SLOT, post)
    col = lax.broadcasted_iota(I32, (n_exp, kw), 1)
    spread = (lax.shift_right_logical(col, shift) == lax.broadcasted_iota(I32, (n_exp, kw), 0)).astype(BF16)
    hi = lax.shift_right_logical(digits, DIGIT_BITS).astype(F32).astype(BF16)
    lo = (digits & ((1 << DIGIT_BITS) - 1)).astype(F32).astype(BF16)
    rel = (float(1 << DIGIT_BITS) * jnp.dot(hi, spread, preferred_element_type=F32)
           + jnp.dot(lo, spread, preferred_element_type=F32))
    col1 = lax.broadcasted_iota(I32, (1, kw), 1)
    owner = lax.shift_right_logical(col1, shift)
    tgt = col1 & (ws - 1)
    for e in range(n_exp):
        tgt = jnp.where(owner == e, tgt + _window_base(off_ref, e, i, offw, cap, ws), tgt)
        window_copy(i, slot, e).wait()
    hit = jnp.where(rel == tgt.astype(F32), 1.0, 0.0).astype(BF16)
    gate2 = mod_ref[0, 5:6, :]
    acc = jnp.dot(hit, win[slot], preferred_element_type=F32)
    xo_ref[...] = x_ref[...] + gate2 * acc
    lane = lax.broadcasted_iota(I32, (tb, ws), 1)

    for e in range(n_exp):
        end = off_ref[e * offw + i + 1]
        base = _window_base(off_ref, e, i, offw, cap, ws)
        n_more = jnp.maximum(end - (base + ws) + ws - 1, 0) // ws

        def more(k, _):
            b2 = jnp.minimum(base + (k + 1) * ws, cap - ws)
            cp = pltpu.make_async_copy(y_hbm.at[pl.ds(e * cap + b2, ws), :], extra, xsem)
            cp.start()
            cp.wait()
            pc = jnp.transpose(pos_ref[...].astype(F32)).astype(I32)[:, e:e + 1]
            h2 = (pc == (b2 + lane)) & (pc >= base + (k + 1) * ws)
            xo_ref[...] += gate2 * jnp.dot(jnp.where(h2, 1.0, 0.0).astype(BF16), extra[...],
                                           preferred_element_type=F32)
            return 0

        lax.fori_loop(0, n_more, more, 0)


def _combine(x2, pos, off, y2, mod, tokens_per_batch, cap, tb, ws):
    N, D = x2.shape
    E = pos.shape[0]
    offw = off.shape[1]
    n_tile = N // tb
    per_b = tokens_per_batch // tb
    bc = mod.shape[0]
    mod_map = (lambda i, off: (i // per_b, 0, 0)) if bc > 1 else (lambda i, off: (0, 0, 0))
    grid_spec = pltpu.PrefetchScalarGridSpec(
        num_scalar_prefetch=1,
        grid=(n_tile,),
        in_specs=[pl.BlockSpec((tb, D), lambda i, off: (i, 0)),
                  pl.BlockSpec((E, tb), lambda i, off: (0, i)),
                  pl.BlockSpec((1, 6, D), mod_map),
                  pl.BlockSpec(memory_space=pl.ANY)],
        out_specs=pl.BlockSpec((tb, D), lambda i, off: (i, 0)),
        scratch_shapes=[pltpu.VMEM((2, E * ws, D), BF16), pltpu.VMEM((ws, D), BF16),
                        pltpu.SemaphoreType.DMA((2, E)), pltpu.SemaphoreType.DMA(())],
    )
    return pl.pallas_call(
        functools.partial(_combine_kernel, n_exp=E, n_tile=n_tile, cap=cap, ws=ws, offw=offw),
        grid_spec=grid_spec,
        out_shape=jax.ShapeDtypeStruct((N, D), F32),
        compiler_params=_cparams(("arbitrary",)),
        name="moe_combine",
    )(off.reshape(-1), x2, pos, mod, y2)


def _moe(x, hn2, logits_t, mod, p, layer):
    B, T, D = x.shape
    N = B * T
    E = logits_t.shape[0]
    cap = EC_FACTOR * N // E
    tb = min(TOKEN_TILE, T)
    ws = min(SLOT_WINDOW, cap)
    per_expert = (cap + ws) * (D * 2 + LANE * 4)
    grp = 1
    while grp * 2 <= E and grp * 2 * per_expert <= STAGING_BYTES:
        grp *= 2
    sb = min(SLOT_BLOCK if grp * per_expert <= STAGING_BYTES // 2 else SLOT_BLOCK // 2, cap)
    pos, g, off = _select(logits_t, cap, tb)
    y2 = _experts(hn2.reshape(N, D), pos, g, off, p["moe_w_gate"], p["moe_w_up"], p["moe_w_down"],
                  layer, cap, tb, sb, ws, grp)
    assert cap <= NO_SLOT // 2
    xo = _combine(x.reshape(N, D), pos, off, y2, mod, T, cap, tb, min(COMBINE_WINDOW, cap))
    return xo.reshape(B, T, D)


def _final_kernel(x_ref, g_ref, o_ref):
    o_ref[...] = _rms(x_ref[...], g_ref[...])


def _final_norm(x, g):
    B, T, D = x.shape
    N = B * T
    tn = min(N, 1024)
    return pl.pallas_call(
        _final_kernel,
        grid=(N // tn,),
        in_specs=[pl.BlockSpec((tn, D), lambda i: (i, 0)), pl.BlockSpec((1, D), lambda i: (0, 0))],
        out_specs=pl.BlockSpec((tn, D), lambda i: (i, 0)),
        out_shape=jax.ShapeDtypeStruct((N, D), F32),
        compiler_params=_cparams(("arbitrary",)),
        name="final_norm",
    )(x.reshape(N, D), g.reshape(1, D)).reshape(B, T, D)


def _grid_pos_embed(n_tokens, d_model):
    rows = n_tokens // GRID_W
    row = jnp.repeat(jnp.arange(rows, dtype=F32), GRID_W)
    col = jnp.tile(jnp.arange(GRID_W, dtype=F32), rows)
    q = d_model // 4
    freq = jnp.exp(-math.log(POS_BASE) * jnp.arange(q, dtype=F32) / q)
    ang_r = row[:, None] * freq
    ang_c = col[:, None] * freq
    return jnp.concatenate([jnp.sin(ang_r), jnp.cos(ang_r), jnp.sin(ang_c), jnp.cos(ang_c)], axis=-1)


def _trunk(x, mods, h0, p):
    depth = p["norm1_g"].shape[0]
    finals = []
    for l in range(depth):
        mod = mods[l]
        router2 = _split_router(p["moe_router"][l])
        j = l // 2
        if l % 2 == 0:
            x, hn2, lg, fin = _lru_layer(x, mod, h0[:, j], p, j, p["norm1_g"][l], p["norm2_g"][l], router2)
            finals.append(fin)
        else:
            x, hn2, lg = _sgu_layer(x, mod, p, j, p["norm1_g"][l], p["norm2_g"][l], router2)
        x = _moe(x, hn2, lg, mod, p, l)
    return _final_norm(x, p["final_norm_g"]), jnp.stack(finals, axis=1)


def kernel(x_prompt, x_sample, state_lru, c, c_ctx, norm1_g, norm2_g, w_mod, b_mod,
           lru_w_in, lru_conv_w, lru_conv_b, lru_w_a, lru_b_a, lru_w_x, lru_b_x, lru_lam, lru_w_out,
           sg_w_in, sg_norm_g, sg_w_s, sg_b_s, sg_w_out,
           moe_router, moe_w_gate, moe_w_up, moe_w_down, final_norm_g):
    p = dict(norm1_g=norm1_g, norm2_g=norm2_g, lru_w_in=lru_w_in, lru_conv_w=lru_conv_w,
             lru_conv_b=lru_conv_b, lru_w_a=lru_w_a, lru_b_a=lru_b_a, lru_w_x=lru_w_x, lru_b_x=lru_b_x,
             lru_lam=lru_lam, lru_w_out=lru_w_out, sg_w_in=sg_w_in, sg_norm_g=sg_norm_g, sg_w_s=sg_w_s,
             sg_b_s=sg_b_s, sg_w_out=sg_w_out, moe_router=moe_router, moe_w_gate=moe_w_gate.astype(BF16),
             moe_w_up=moe_w_up.astype(BF16), moe_w_down=moe_w_down.astype(BF16), final_norm_g=final_norm_g)
    L, D, _ = w_mod.shape
    bs = c.shape[0]
    n_lru, _, W = lru_lam.shape

    rows = -(-(1 + bs) // SUBLANE) * SUBLANE
    cond = jnp.zeros((rows, D), F32).at[0].set(c_ctx).at[1:1 + bs].set(c)
    mods = _modulation(cond, w_mod, b_mod).reshape(L, rows, 6, D)

    h0_ctx = jnp.zeros((x_prompt.shape[0], n_lru, 2, W), F32)
    y_prompt, new_state = _trunk(x_prompt, mods[:, 0:1], h0_ctx, p)

    xs = _add_pos(x_sample, _grid_pos_embed(x_sample.shape[1], D))
    y_sample, _ = _trunk(xs, mods[:, 1:1 + bs], state_lru, p)
    return (y_prompt, y_sample, new_state)
```

```python
import functools
import math

import jax
import jax.numpy as jnp
from jax import lax
from jax.experimental import pallas as pl
from jax.experimental.pallas import tpu as pltpu

F32 = jnp.float32
BF16 = jnp.bfloat16
I32 = jnp.int32
HIGHEST = lax.Precision.HIGHEST

RMS_EPS = 1e-6
LRU_C = 8.0
CONV_W = 4
CHUNK = 128
SG_GROUPS = 8
GRID_W = 64
POS_BASE = 10000.0
EC_FACTOR = 2
LANE = 128
SUBLANE = 8
BF16_ROWS = 16
VMEM_LIMIT = 56 * 1024 * 1024

TOKEN_TILE = 256
SLOT_BLOCK = 1024
GATHER_TOKENS = 2048
LRU_ROWS = 512
STAGING_BYTES = 22 * 1024 * 1024
COMBINE_WINDOW = 128
DIGIT_BITS = 6
NO_SLOT = 1 << 14
SLOT_WINDOW = 64


def _cparams(sem):
    return pltpu.CompilerParams(dimension_semantics=sem, vmem_limit_bytes=VMEM_LIMIT)


def _rms(x, g):
    return x * lax.rsqrt(jnp.mean(x * x, axis=-1, keepdims=True) + RMS_EPS) * g


def _gelu_tanh(x):
    c = math.sqrt(2.0 / math.pi)
    return x * (0.5 * (1.0 + jnp.tanh(c * (x + 0.044715 * (x * x * x)))))


def _sigmoid(x):
    return 1.0 / (1.0 + jnp.exp(-x))


def _log1p(e):
    w = 1.0 + e
    return jnp.where(w == 1.0, e, e * jnp.log(w) / jnp.where(w == 1.0, 1.0, w - 1.0))


def _softplus(x):
    return jnp.maximum(x, 0.0) + _log1p(jnp.exp(-jnp.abs(x)))


def _neg_expm1_2x(x):
    t = jnp.tanh(x)
    return (-2.0 * t) / (1.0 - t)


def _mod_kernel(c_ref, w_ref, b_ref, o_ref):
    c = c_ref[...]
    sc = c * _sigmoid(c)
    o_ref[0] = jnp.dot(sc, w_ref[0], preferred_element_type=F32, precision=HIGHEST) + b_ref[0]


def _modulation(cond, w_mod, b_mod):
    L, D, D6 = w_mod.shape
    R = cond.shape[0]
    tn = D6 // 4
    return pl.pallas_call(
        _mod_kernel,
        grid=(L, D6 // tn),
        in_specs=[pl.BlockSpec((R, D), lambda l, n: (0, 0)),
                  pl.BlockSpec((1, D, tn), lambda l, n: (l, 0, n)),
                  pl.BlockSpec((1, 1, tn), lambda l, n: (l, 0, n))],
        out_specs=pl.BlockSpec((1, R, tn), lambda l, n: (l, 0, n)),
        out_shape=jax.ShapeDtypeStruct((L, R, D6), F32),
        compiler_params=_cparams(("arbitrary", "arbitrary")),
        name="modulation",
    )(cond, w_mod, b_mod.reshape(L, 1, D6))


def _add_kernel(x_ref, p_ref, o_ref):
    o_ref[0] = x_ref[0] + p_ref[...]


def _add_pos(x, pe):
    B, T, D = x.shape
    tt = min(T, 512)
    return pl.pallas_call(
        _add_kernel,
        grid=(T // tt, B),
        in_specs=[pl.BlockSpec((1, tt, D), lambda t, b: (b, t, 0)),
                  pl.BlockSpec((tt, D), lambda t, b: (t, 0))],
        out_specs=pl.BlockSpec((1, tt, D), lambda t, b: (b, t, 0)),
        out_shape=jax.ShapeDtypeStruct((B, T, D), F32),
        compiler_params=_cparams(("arbitrary", "arbitrary")),
        name="add_pos",
    )(x, pe)


def _lru_gates(xb, wg_ref, ba, bx, sp, a_scr, u_scr):
    heads = wg_ref.shape[0]
    for h in range(heads):
        hs = slice(h * LANE, (h + 1) * LANE)
        xh = xb[:, hs]
        z = jnp.dot(xh.astype(BF16), wg_ref[h], preferred_element_type=F32)
        r = _sigmoid(z[:, :LANE] + ba[:, hs])
        i = _sigmoid(z[:, LANE:] + bx[:, hs])
        log_a = (-LRU_C) * r * sp[:, hs]
        a = jnp.exp(log_a)
        u = jnp.sqrt(_neg_expm1_2x(log_a)) * (i * xh)
        a_scr[:, hs] = a
        u_scr[:, hs] = u


def _lru_fwd_kernel(x_ref, xn_ref, mod_ref, g_ref, win_ref, cw_ref, cb_ref, wg_ref,
                    ba_ref, bx_ref, lam_ref, h0_ref,
                    gate_ref, xb_ref, hf_ref, fin_ref,
                    carry_ref, prev_scr, a_scr, u_scr, *, tt, n_t, width):
    t = pl.program_id(0)
    nb = x_ref.shape[0]
    rows = tt * nb

    @pl.when(t == 0)
    def _():
        carry_ref[...] = h0_ref[...]
        prev_scr[...] = jnp.zeros(prev_scr.shape, F32)

    ext = jnp.concatenate([jnp.swapaxes(x_ref[...], 0, 1), xn_ref[:, 0, :][None]], axis=0)
    hn = _rms(ext, g_ref[...]) * (1.0 + mod_ref[:, 1, :][None]) + mod_ref[:, 0, :][None]
    hn = hn.reshape(rows + nb, hn.shape[-1]).astype(BF16)
    proj = jnp.dot(hn, win_ref[...], preferred_element_type=F32)
    gate_ref[0] = _gelu_tanh(proj[:rows, :width])

    ahead = jnp.where(t < n_t - 1, proj[rows:, width:], 0.0)
    span = jnp.concatenate([prev_scr[...], proj[:rows, width:], ahead], axis=0)
    prev_scr[...] = proj[rows - 2 * nb:rows, width:]
    xb = cb_ref[...]
    for k in range(CONV_W):
        xb = xb + cw_ref[k:k + 1, :] * span[k * nb:k * nb + rows, :]
    xb_ref[0] = xb

    sp = _softplus(-lam_ref[...])
    _lru_gates(xb, wg_ref, ba_ref[...], bx_ref[...], sp, a_scr, u_scr)

    def body(ti, h):
        r0 = pl.multiple_of(ti * nb, nb)
        h = u_scr[pl.ds(r0, nb), :] + a_scr[pl.ds(r0, nb), :] * h
        hf_ref[0, pl.ds(r0, nb), :] = h
        return h

    h = lax.fori_loop(0, tt, body, carry_ref[...])
    carry_ref[...] = h
    fin_ref[...] = h


def _lru_bwd_kernel(x_ref, gate_ref, xb_ref, hf_ref, mod_ref, wg_ref, ba_ref, bx_ref, lam_ref, h0_ref,
                    wout_ref, g2_ref, r_ref,
                    xo_ref, hn_ref, lg_ref, fin_ref,
                    carry_ref, a_scr, u_scr, hb_scr, *, tt):
    t = pl.program_id(0)
    nb = x_ref.shape[0]

    @pl.when(t == 0)
    def _():
        carry_ref[...] = h0_ref[...]

    sp = _softplus(-lam_ref[...])
    _lru_gates(xb_ref[0], wg_ref, ba_ref[...], bx_ref[...], sp, a_scr, u_scr)

    def body(ti, h):
        r0 = pl.multiple_of((tt - 1 - ti) * nb, nb)
        h = u_scr[pl.ds(r0, nb), :] + a_scr[pl.ds(r0, nb), :] * h
        hb_scr[pl.ds(r0, nb), :] = h
        return h

    h = lax.fori_loop(0, tt, body, carry_ref[...])
    carry_ref[...] = h
    fin_ref[...] = h

    y = ((hf_ref[0] + hb_scr[...]) * gate_ref[0]).astype(BF16)
    y = jnp.dot(y, wout_ref[...], preferred_element_type=F32)
    y = jnp.swapaxes(y.reshape(tt, nb, y.shape[-1]), 0, 1)
    xo = x_ref[...] + mod_ref[:, 2:3, :] * y
    xo_ref[...] = xo
    hn = _rms(xo, g2_ref[...]) * (1.0 + mod_ref[:, 4:5, :]) + mod_ref[:, 3:4, :]
    hn_ref[...] = hn.astype(BF16)
    lg = _router_logits(hn.reshape(nb * tt, hn.shape[-1]), r_ref)
    lg_ref[...] = lg.reshape(nb, tt, lg.shape[-1])


def _router_logits(hn, r_ref):
    h_hi = hn.astype(BF16)
    h_lo = (hn - h_hi.astype(F32)).astype(BF16)
    return (jnp.dot(h_hi, r_ref[0], preferred_element_type=F32)
            + jnp.dot(h_hi, r_ref[1], preferred_element_type=F32)
            + jnp.dot(h_lo, r_ref[0], preferred_element_type=F32))


def _split_router(router):
    hi = router.astype(BF16)
    lo = (router - hi.astype(F32)).astype(BF16)
    return jnp.stack([hi, lo])


def _moe_prenorm(xo, mod_ref, g2_ref, r_ref, hn_ref, lg_ref):
    hn = _rms(xo, g2_ref[...]) * (1.0 + mod_ref[0, 4:5, :]) + mod_ref[0, 3:4, :]
    hn_ref[0] = hn.astype(BF16)
    lg_ref[0] = _router_logits(hn, r_ref)


def _lru_layer(x, mod, h0, p, j, norm1_g, norm2_g, router2):
    B, T, D = x.shape
    W = p["lru_lam"].shape[-1]
    H = W // LANE
    E = router2.shape[-1]
    tt = min(T, max(SUBLANE, LRU_ROWS // B))
    n_t = T // tt
    rows = tt * B
    bc = mod.shape[0]
    assert B % SUBLANE == 0 and T % tt == 0 and tt % SUBLANE == 0 and bc in (1, B)
    n8 = T // SUBLANE
    r8 = tt // SUBLANE

    def wg(d):
        return jnp.concatenate([p["lru_w_a"][j, d], p["lru_w_x"][j, d]], axis=-1).astype(BF16)

    def vec(name, d):
        return p[name][j, d].reshape(1, W)

    full = lambda shape: pl.BlockSpec(shape, lambda t: (0,) * len(shape), pipeline_mode=pl.Buffered(1))
    fwd_x = lambda t: (0, t, 0)
    fwd_w = lambda t: (t, 0, 0)
    tile_w = lambda imap: pl.BlockSpec((1, rows, W), imap)
    inner = jax.ShapeDtypeStruct((n_t, rows, W), F32)
    gate, xb, hf, fin_f = pl.pallas_call(
        functools.partial(_lru_fwd_kernel, tt=tt, n_t=n_t, width=W),
        grid=(n_t,),
        in_specs=[pl.BlockSpec((B, tt, D), fwd_x),
                  pl.BlockSpec((B, SUBLANE, D), lambda t: (0, jnp.minimum((t + 1) * r8, n8 - 1), 0)),
                  full((bc, 6, D)),
                  full((1, D)), full((D, 2 * W)), full((CONV_W, W)), full((1, W)),
                  full((H, LANE, 2 * LANE)), full((1, W)), full((1, W)), full((1, W)),
                  full((B, W))],
        out_specs=[tile_w(fwd_w), tile_w(fwd_w), tile_w(fwd_w), pl.BlockSpec((B, W), lambda t: (0, 0))],
        out_shape=[inner] * 3 + [jax.ShapeDtypeStruct((B, W), F32)],
        scratch_shapes=[pltpu.VMEM((B, W), F32), pltpu.VMEM((2 * B, W), F32),
                        pltpu.VMEM((rows, W), F32), pltpu.VMEM((rows, W), F32)],
        compiler_params=_cparams(("arbitrary",)),
        name="lru_fwd",
    )(x, x, mod, norm1_g.reshape(1, D), p["lru_w_in"][j].astype(BF16), p["lru_conv_w"][j],
      p["lru_conv_b"][j].reshape(1, W), wg(0), vec("lru_b_a", 0), vec("lru_b_x", 0), vec("lru_lam", 0),
      h0[:, 0])

    bwd_x = lambda t: (0, n_t - 1 - t, 0)
    bwd_w = lambda t: (n_t - 1 - t, 0, 0)
    xo, hn2, lg, fin_b = pl.pallas_call(
        functools.partial(_lru_bwd_kernel, tt=tt),
        grid=(n_t,),
        in_specs=[pl.BlockSpec((B, tt, D), bwd_x), tile_w(bwd_w), tile_w(bwd_w), tile_w(bwd_w),
                  full((bc, 6, D)),
                  full((H, LANE, 2 * LANE)), full((1, W)), full((1, W)), full((1, W)),
                  full((B, W)),
                  full((W, D)), full((1, D)), full((2, D, E))],
        out_specs=[pl.BlockSpec((B, tt, D), bwd_x), pl.BlockSpec((B, tt, D), bwd_x),
                   pl.BlockSpec((B, tt, E), bwd_x), pl.BlockSpec((B, W), lambda t: (0, 0))],
        out_shape=[jax.ShapeDtypeStruct((B, T, D), F32), jax.ShapeDtypeStruct((B, T, D), BF16),
                   jax.ShapeDtypeStruct((B, T, E), F32), jax.ShapeDtypeStruct((B, W), F32)],
        scratch_shapes=[pltpu.VMEM((B, W), F32), pltpu.VMEM((rows, W), F32), pltpu.VMEM((rows, W), F32),
                        pltpu.VMEM((rows, W), F32)],
        compiler_params=_cparams(("arbitrary",)),
        name="lru_bwd",
    )(x, gate, xb, hf, mod, wg(1), vec("lru_b_a", 1), vec("lru_b_x", 1), vec("lru_lam", 1), h0[:, 1],
      p["lru_w_out"][j].astype(BF16), norm2_g.reshape(1, D), router2)
    return xo, hn2, lg.reshape(B * T, E).T, jnp.stack([fin_f, fin_b], axis=1)


def _sgu_kernel(x_ref, mod_ref, g_ref, win_ref, ng_ref, ws_ref, bs_ref, wout_ref, g2_ref, rt_ref,
                xo_ref, hn_ref, lg_ref, u_scr, v_scr, p_scr, *, tt, sgw):
    x = x_ref[0]
    hn = (_rms(x, g_ref[...]) * (1.0 + mod_ref[0, 1:2, :]) + mod_ref[0, 0:1, :]).astype(BF16)
    u_scr[...] = _gelu_tanh(jnp.dot(hn, win_ref[:, :sgw], preferred_element_type=F32))
    v = _gelu_tanh(jnp.dot(hn, win_ref[:, sgw:], preferred_element_type=F32))
    v_scr[...] = _rms(v, ng_ref[...]).astype(BF16)
    gd = sgw // SG_GROUPS
    for n in range(tt // CHUNK):
        rs = slice(n * CHUNK, (n + 1) * CHUNK)
        for g in range(SG_GROUPS):
            cs = slice(g * gd, (g + 1) * gd)
            sv = jnp.dot(ws_ref[g], v_scr[rs, cs], preferred_element_type=F32) + bs_ref[:, g:g + 1]
            p_scr[rs, cs] = (u_scr[rs, cs] * sv).astype(BF16)
    y = jnp.dot(p_scr[...], wout_ref[...], preferred_element_type=F32)
    xo = x + mod_ref[0, 2:3, :] * y
    xo_ref[0] = xo
    _moe_prenorm(xo, mod_ref, g2_ref, rt_ref, hn_ref, lg_ref)


def _sgu_layer(x, mod, p, j, norm1_g, norm2_g, router2):
    B, T, D = x.shape
    sgw = p["sg_norm_g"].shape[-1]
    E = router2.shape[-1]
    tt = min(T, 512)
    n_t = T // tt
    bc = mod.shape[0]
    mod_map = (lambda b, t: (b, 0, 0)) if bc > 1 else (lambda b, t: (0, 0, 0))
    full = lambda shape: pl.BlockSpec(shape, lambda b, t: (0,) * len(shape), pipeline_mode=pl.Buffered(1))
    tile = pl.BlockSpec((1, tt, D), lambda b, t: (b, t, 0))
    xo, hn2, lg = pl.pallas_call(
        functools.partial(_sgu_kernel, tt=tt, sgw=sgw),
        grid=(B, n_t),
        in_specs=[tile, pl.BlockSpec((1, 6, D), mod_map), full((1, D)), full((D, 2 * sgw)), full((1, sgw)),
                  full((SG_GROUPS, CHUNK, CHUNK)), full((CHUNK, SG_GROUPS)), full((sgw, D)),
                  full((1, D)), full((2, D, E))],
        out_specs=[tile, tile, pl.BlockSpec((1, tt, E), lambda b, t: (b, t, 0))],
        out_shape=[jax.ShapeDtypeStruct((B, T, D), F32), jax.ShapeDtypeStruct((B, T, D), BF16),
                   jax.ShapeDtypeStruct((B, T, E), F32)],
        scratch_shapes=[pltpu.VMEM((tt, sgw), F32), pltpu.VMEM((tt, sgw), BF16), pltpu.VMEM((tt, sgw), BF16)],
        compiler_params=_cparams(("arbitrary", "arbitrary")),
        name="sgu",
    )(x, mod, norm1_g.reshape(1, D), p["sg_w_in"][j].astype(BF16), p["sg_norm_g"][j].reshape(1, sgw),
      p["sg_w_s"][j].astype(BF16), p["sg_b_s"][j].T, p["sg_w_out"][j].astype(BF16),
      norm2_g.reshape(1, D), router2)
    return xo, hn2, lg.reshape(B * T, E).T


def _select_kernel(lg_ref, pos_ref, g_ref, off_ref, aff_scr, *, cap, tb):
    E, N = lg_ref.shape
    n_tile = N // tb
    lg = lg_ref[...]
    ex = jnp.exp(lg - jnp.max(lg, axis=0, keepdims=True))
    aff_scr[...] = ex / jnp.sum(ex, axis=0, keepdims=True)

    def search(it, cur):
        cand = cur | (1 << (30 - it))
        bits = pltpu.bitcast(aff_scr[...], I32)
        cnt = jnp.sum(jnp.where(bits >= cand, 1.0, 0.0), axis=1, keepdims=True)
        return jnp.where(cnt >= cap, cand, cur)

    thr = lax.fori_loop(0, 31, search, jnp.zeros((E, 1), I32))
    bits = pltpu.bitcast(aff_scr[...], I32)
    n_gt = jnp.sum(jnp.where(bits > thr, 1.0, 0.0), axis=1, keepdims=True)
    need = cap - n_gt

    tri = (lax.broadcasted_iota(I32, (tb, tb), 0) <= lax.broadcasted_iota(I32, (tb, tb), 1)).astype(BF16)
    lane = lax.broadcasted_iota(I32, off_ref.shape, 1)

    def chunk(c, carry):
        c_eq, c_pos = carry
        l0 = pl.multiple_of(c * tb, tb)
        aff = aff_scr[:, pl.ds(l0, tb)]
        b = pltpu.bitcast(aff, I32)
        eq = b == thr
        eqf = jnp.where(eq, 1.0, 0.0)
        rank = jnp.dot(eqf.astype(BF16), tri, preferred_element_type=F32) - eqf + c_eq
        sel = (b > thr) | (eq & (rank < need))
        self_ = jnp.where(sel, 1.0, 0.0)
        inc = jnp.dot(self_.astype(BF16), tri, preferred_element_type=F32)
        pos = inc - self_ + c_pos
        pos_ref[:, pl.ds(l0, tb)] = jnp.where(sel, pos, -1.0).astype(I32)
        g_ref[:, pl.ds(l0, tb)] = jnp.where(sel, aff, 0.0)
        off_ref[...] = jnp.where(lane == c, jnp.broadcast_to(c_pos, off_ref.shape).astype(I32), off_ref[...])
        return (c_eq + jnp.sum(eqf, axis=1, keepdims=True), c_pos + inc[:, tb - 1:tb])

    off_ref[...] = jnp.zeros(off_ref.shape, I32)
    zero = jnp.zeros((E, 1), F32)
    _, total = lax.fori_loop(0, n_tile, chunk, (zero, zero))
    off_ref[...] = jnp.where(lane == n_tile, jnp.broadcast_to(total, off_ref.shape).astype(I32), off_ref[...])


def _select(logits_t, cap, tb):
    E, N = logits_t.shape
    offw = -(-(N // tb + 1) // LANE) * LANE
    return pl.pallas_call(
        functools.partial(_select_kernel, cap=cap, tb=tb),
        out_shape=[jax.ShapeDtypeStruct((E, N), I32), jax.ShapeDtypeStruct((E, N), F32),
                   jax.ShapeDtypeStruct((E, offw), I32)],
        scratch_shapes=[pltpu.VMEM((E, N), F32)],
        compiler_params=pltpu.CompilerParams(vmem_limit_bytes=VMEM_LIMIT),
        name="moe_select",
    )(logits_t)


def _expert_kernel(off_ref, x_ref, pos_ref, wg_ref, wu_ref, wd_ref, y_ref,
                   xs_scr, *, n_gather, n_blk, tb, sb, ws, offw):
    s = pl.program_id(0)
    d = pl.program_id(1)
    sub = x_ref.shape[1] // tb
    grp = xs_scr.shape[0]
    row_iota = lax.broadcasted_iota(I32, (ws, tb), 0)

    @pl.when(d == 0)
    def _():
        xs_scr[...] = jnp.zeros(xs_scr.shape, xs_scr.dtype)

    @pl.when(d < n_gather)
    def _gather():
        for j in range(sub):
            i = d * sub + j
            xt = x_ref[0, j * tb:(j + 1) * tb, :]
            lanes = slice(j * tb, (j + 1) * tb)
            bases, hits = [], []
            for k in range(grp):
                e = s * grp + k
                o0 = off_ref[e * offw + i]
                base = pl.multiple_of((o0 // BF16_ROWS) * BF16_ROWS, BF16_ROWS)
                bases.append(base)
                hits.append(pos_ref[k, 0, :, lanes] == (base + row_iota))
            stacked = jnp.where(jnp.concatenate(hits, axis=0), 1.0, 0.0).astype(BF16)
            rows = jnp.dot(stacked, xt, preferred_element_type=F32)
            for k in range(grp):
                xs_scr[k, pl.ds(bases[k], ws), :] += rows[k * ws:(k + 1) * ws, :].astype(BF16)
            for k in range(grp):
                e = s * grp + k
                end = off_ref[e * offw + i + 1]
                n_more = jnp.maximum(end - (bases[k] + ws) + ws - 1, 0) // ws

                def more(ch, _):
                    r0 = pl.multiple_of(bases[k] + (ch + 1) * ws, BF16_ROWS)
                    hit = pos_ref[k, 0, :, lanes] == (r0 + row_iota)
                    extra = jnp.dot(jnp.where(hit, 1.0, 0.0).astype(BF16), xt, preferred_element_type=F32)
                    xs_scr[k, pl.ds(r0, ws), :] += extra.astype(BF16)
                    return 0

                lax.fori_loop(0, n_more, more, 0)

    @pl.when(d >= n_gather)
    def _ffn():
        step = d - n_gather
        k = step // n_blk
        r0 = pl.multiple_of((step - k * n_blk) * sb, sb)
        xb = xs_scr[k, pl.ds(r0, sb), :]
        hg = jnp.dot(xb, wg_ref[0, 0], preferred_element_type=F32)
        hu = jnp.dot(xb, wu_ref[0, 0], preferred_element_type=F32)
        h = (hg * _sigmoid(hg) * hu).astype(BF16)
        o = jnp.dot(h, wd_ref[0, 0], preferred_element_type=F32)
        y_ref[...] = o.astype(BF16)


def _experts(hn2, pos, off, w_gate, w_up, w_down, layer, cap, tb, sb, ws, grp):
    N, D = hn2.shape
    E = pos.shape[0]
    F = w_gate.shape[-1]
    offw = off.shape[1]
    n_blk = cap // sb
    ns = min(N, GATHER_TOKENS)
    n_gather = N // ns
    assert ns % tb == 0 and cap % sb == 0 and N % ns == 0 and E % grp == 0
    tok = lambda s, d, off: jnp.minimum(d, n_gather - 1)
    ffn = lambda d: jnp.maximum(d - n_gather, 0)
    emap = lambda s, d, off: (s, tok(s, d, off), 0, 0)
    wmap = lambda s, d, off: (layer, s * grp + ffn(d) // n_blk, 0, 0)
    grid_spec = pltpu.PrefetchScalarGridSpec(
        num_scalar_prefetch=1,
        grid=(E // grp, n_gather + grp * n_blk),
        in_specs=[pl.BlockSpec((1, ns, D), lambda s, d, off: (tok(s, d, off), 0, 0)),
                  pl.BlockSpec((grp, 1, 1, ns), emap),
                  pl.BlockSpec((1, 1, D, F), wmap), pl.BlockSpec((1, 1, D, F), wmap),
                  pl.BlockSpec((1, 1, F, D), wmap)],
        out_specs=pl.BlockSpec((sb, D), lambda s, d, off: (s * grp * n_blk + ffn(d), 0)),
        scratch_shapes=[pltpu.VMEM((grp, cap + ws, D), BF16)],
    )
    return pl.pallas_call(
        functools.partial(_expert_kernel, n_gather=n_gather, n_blk=n_blk, tb=tb, sb=sb, ws=ws, offw=offw),
        grid_spec=grid_spec,
        out_shape=jax.ShapeDtypeStruct((E * cap, D), BF16),
        compiler_params=_cparams(("arbitrary", "arbitrary")),
        name="moe_experts",
    )(off.reshape(-1), hn2.reshape(n_gather, ns, D), pos.reshape(E, n_gather, 1, ns), w_gate, w_up, w_down)


def _window_base(off_ref, e, i, offw, cap, ws):
    o0 = off_ref[e * offw + i]
    return jnp.minimum((o0 // BF16_ROWS) * BF16_ROWS, cap - ws)


def _combine_kernel(off_ref, x_ref, pos_ref, g_ref, mod_ref, fg_ref, y_hbm, xo_ref, win, extra, sem, xsem,
                    *, n_exp, n_tile, cap, ws, offw, final_norm):
    i = pl.program_id(0)
    slot = i % 2
    tb = x_ref.shape[0]

    def window_copy(ti, sl, e):
        base = _window_base(off_ref, e, ti, offw, cap, ws)
        return pltpu.make_async_copy(y_hbm.at[pl.ds(e * cap + base, ws), :],
                                     win.at[sl, pl.ds(e * ws, ws), :], sem.at[sl, e])

    @pl.when(i == 0)
    def _():
        for e in range(n_exp):
            window_copy(0, 0, e).start()

    @pl.when(i + 1 < n_tile)
    def _():
        for e in range(n_exp):
            window_copy(i + 1, 1 - slot, e).start()

    kw = n_exp * ws
    gate_t = jnp.transpose(g_ref[...]).astype(BF16)
    shift = ws.bit_length() - 1
    post = jnp.transpose(pos_ref[...].astype(F32)).astype(I32)
    digits = jnp.where(post < 0, NO_SLOT, post)
    col = lax.broadcasted_iota(I32, (n_exp, kw), 1)
    spread = (lax.shift_right_logical(col, shift) == lax.broadcasted_iota(I32, (n_exp, kw), 0)).astype(BF16)
    hi = lax.shift_right_logical(digits, DIGIT_BITS).astype(F32).astype(BF16)
    lo = (digits & ((1 << DIGIT_BITS) - 1)).astype(F32).astype(BF16)
    rel = (float(1 << DIGIT_BITS) * jnp.dot(hi, spread, preferred_element_type=F32)
           + jnp.dot(lo, spread, preferred_element_type=F32))
    col1 = lax.broadcasted_iota(I32, (1, kw), 1)
    owner = lax.shift_right_logical(col1, shift)
    tgt = col1 & (ws - 1)
    for e in range(n_exp):
        tgt = jnp.where(owner == e, tgt + _window_base(off_ref, e, i, offw, cap, ws), tgt)
        window_copy(i, slot, e).wait()
    weight = jnp.dot(gate_t, spread, preferred_element_type=F32)
    hit = jnp.where(rel == tgt.astype(F32), weight, 0.0).astype(BF16)
    gate2 = mod_ref[0, 5:6, :]
    acc = jnp.dot(hit, win[slot], preferred_element_type=F32)
    xo_ref[...] = x_ref[...] + gate2 * acc
    lane = lax.broadcasted_iota(I32, (tb, ws), 1)

    for e in range(n_exp):
        end = off_ref[e * offw + i + 1]
        base = _window_base(off_ref, e, i, offw, cap, ws)
        n_more = jnp.maximum(end - (base + ws) + ws - 1, 0) // ws

        def more(k, _):
            b2 = jnp.minimum(base + (k + 1) * ws, cap - ws)
            cp = pltpu.make_async_copy(y_hbm.at[pl.ds(e * cap + b2, ws), :], extra, xsem)
            cp.start()
            cp.wait()
            pc = jnp.transpose(pos_ref[...].astype(F32)).astype(I32)[:, e:e + 1]
            h2 = (pc == (b2 + lane)) & (pc >= base + (k + 1) * ws)
            wcol = jnp.transpose(g_ref[...]).astype(BF16).astype(F32)[:, e:e + 1]
            xo_ref[...] += gate2 * jnp.dot(jnp.where(h2, wcol, 0.0).astype(BF16), extra[...],
                                           preferred_element_type=F32)
            return 0

        lax.fori_loop(0, n_more, more, 0)

    if final_norm:
        xo_ref[...] = _rms(xo_ref[...], fg_ref[...])


def _combine(x2, pos, g, off, y2, mod, tokens_per_batch, cap, tb, ws, final_g):
    N, D = x2.shape
    E = pos.shape[0]
    offw = off.shape[1]
    n_tile = N // tb
    per_b = tokens_per_batch // tb
    bc = mod.shape[0]
    mod_map = (lambda i, off: (i // per_b, 0, 0)) if bc > 1 else (lambda i, off: (0, 0, 0))
    grid_spec = pltpu.PrefetchScalarGridSpec(
        num_scalar_prefetch=1,
        grid=(n_tile,),
        in_specs=[pl.BlockSpec((tb, D), lambda i, off: (i, 0)),
                  pl.BlockSpec((E, tb), lambda i, off: (0, i)),
                  pl.BlockSpec((E, tb), lambda i, off: (0, i)),
                  pl.BlockSpec((1, 6, D), mod_map),
                  pl.BlockSpec((1, D), lambda i, off: (0, 0)),
                  pl.BlockSpec(memory_space=pl.ANY)],
        out_specs=pl.BlockSpec((tb, D), lambda i, off: (i, 0)),
        scratch_shapes=[pltpu.VMEM((2, E * ws, D), BF16), pltpu.VMEM((ws, D), BF16),
                        pltpu.SemaphoreType.DMA((2, E)), pltpu.SemaphoreType.DMA(())],
    )
    return pl.pallas_call(
        functools.partial(_combine_kernel, n_exp=E, n_tile=n_tile, cap=cap, ws=ws, offw=offw,
                          final_norm=final_g is not None),
        grid_spec=grid_spec,
        out_shape=jax.ShapeDtypeStruct((N, D), F32),
        compiler_params=_cparams(("arbitrary",)),
        name="moe_combine",
    )(off.reshape(-1), x2, pos, g, mod, (mod[0, 0] if final_g is None else final_g).reshape(1, D), y2)


def _moe(x, hn2, logits_t, mod, p, layer, final_g):
    B, T, D = x.shape
    N = B * T
    E = logits_t.shape[0]
    cap = EC_FACTOR * N // E
    tb = min(TOKEN_TILE, T)
    ws = min(SLOT_WINDOW, cap)
    per_expert = (cap + ws) * D * 2
    grp = 1
    while grp * 2 <= E and grp * 2 * per_expert <= STAGING_BYTES:
        grp *= 2
    sb = min(SLOT_BLOCK, cap)
    pos, g, off = _select(logits_t, cap, tb)
    y2 = _experts(hn2.reshape(N, D), pos, off, p["moe_w_gate"], p["moe_w_up"], p["moe_w_down"],
                  layer, cap, tb, sb, ws, grp)
    assert cap <= NO_SLOT // 2
    xo = _combine(x.reshape(N, D), pos, g, off, y2, mod, T, cap, tb, min(COMBINE_WINDOW, cap), final_g)
    return xo.reshape(B, T, D)


def _grid_pos_embed(n_tokens, d_model):
    rows = n_tokens // GRID_W
    row = jnp.repeat(jnp.arange(rows, dtype=F32), GRID_W)
    col = jnp.tile(jnp.arange(GRID_W, dtype=F32), rows)
    q = d_model // 4
    freq = jnp.exp(-math.log(POS_BASE) * jnp.arange(q, dtype=F32) / q)
    ang_r = row[:, None] * freq
    ang_c = col[:, None] * freq
    return jnp.concatenate([jnp.sin(ang_r), jnp.cos(ang_r), jnp.sin(ang_c), jnp.cos(ang_c)], axis=-1)


def _trunk(x, mods, h0, p):
    depth = p["norm1_g"].shape[0]
    finals = []
    for l in range(depth):
        mod = mods[l]
        router2 = _split_router(p["moe_router"][l])
        j = l // 2
        if l % 2 == 0:
            x, hn2, lg, fin = _lru_layer(x, mod, h0[:, j], p, j, p["norm1_g"][l], p["norm2_g"][l], router2)
            finals.append(fin)
        else:
            x, hn2, lg = _sgu_layer(x, mod, p, j, p["norm1_g"][l], p["norm2_g"][l], router2)
        x = _moe(x, hn2, lg, mod, p, l, p["final_norm_g"] if l == depth - 1 else None)
    return x, jnp.stack(finals, axis=1)


def kernel(x_prompt, x_sample, state_lru, c, c_ctx, norm1_g, norm2_g, w_mod, b_mod,
           lru_w_in, lru_conv_w, lru_conv_b, lru_w_a, lru_b_a, lru_w_x, lru_b_x, lru_lam, lru_w_out,
           sg_w_in, sg_norm_g, sg_w_s, sg_b_s, sg_w_out,
           moe_router, moe_w_gate, moe_w_up, moe_w_down, final_norm_g):
    p = dict(norm1_g=norm1_g, norm2_g=norm2_g, lru_w_in=lru_w_in, lru_conv_w=lru_conv_w,
             lru_conv_b=lru_conv_b, lru_w_a=lru_w_a, lru_b_a=lru_b_a, lru_w_x=lru_w_x, lru_b_x=lru_b_x,
             lru_lam=lru_lam, lru_w_out=lru_w_out, sg_w_in=sg_w_in, sg_norm_g=sg_norm_g, sg_w_s=sg_w_s,
             sg_b_s=sg_b_s, sg_w_out=sg_w_out, moe_router=moe_router, moe_w_gate=moe_w_gate.astype(BF16),
             moe_w_up=moe_w_up.astype(BF16), moe_w_down=moe_w_down.astype(BF16), final_norm_g=final_norm_g)
    L, D, _ = w_mod.shape
    bs = c.shape[0]
    n_lru, _, W = lru_lam.shape

    rows = -(-(1 + bs) // SUBLANE) * SUBLANE
    cond = jnp.zeros((rows, D), F32).at[0].set(c_ctx).at[1:1 + bs].set(c)
    mods = _modulation(cond, w_mod, b_mod).reshape(L, rows, 6, D)

    h0_ctx = jnp.zeros((x_prompt.shape[0], n_lru, 2, W), F32)
    y_prompt, new_state = _trunk(x_prompt, mods[:, 0:1], h0_ctx, p)

    xs = _add_pos(x_sample, _grid_pos_embed(x_sample.shape[1], D))
    y_sample, _ = _trunk(xs, mods[:, 1:1 + bs], state_lru, p)
    return (y_prompt, y_sample, new_state)
```

```python
import functools
import math

import jax
import jax.numpy as jnp
from jax import lax
from jax.experimental import pallas as pl
from jax.experimental.pallas import tpu as pltpu

F32 = jnp.float32
BF16 = jnp.bfloat16
I32 = jnp.int32
HIGHEST = lax.Precision.HIGHEST

RMS_EPS = 1e-6
F32_TINY = 1e-30
LRU_C = 8.0
CONV_W = 4
CHUNK = 128
SG_GROUPS = 8
GRID_W = 64
POS_BASE = 10000.0
EC_FACTOR = 2
LANE = 128
SUBLANE = 8
BF16_ROWS = 16
VMEM_LIMIT = 56 * 1024 * 1024

TOKEN_TILE = 256
SLOT_BLOCK = 1024
GATHER_TOKENS = 2048
LRU_ROWS = 512
SGU_ROWS = 512
STAGING_BYTES = 22 * 1024 * 1024
COMBINE_WINDOW = 128
DIGIT_BITS = 6
NO_SLOT = 1 << 14
SLOT_WINDOW = 64


def _cparams(sem):
    return pltpu.CompilerParams(dimension_semantics=sem, vmem_limit_bytes=VMEM_LIMIT)


def _rms(x, g):
    return x * lax.rsqrt(jnp.mean(x * x, axis=-1, keepdims=True) + RMS_EPS) * g


def _gelu_tanh(x):
    c = math.sqrt(2.0 / math.pi)
    hx = 0.5 * x
    return hx + hx * jnp.tanh(x * (c + (c * 0.044715) * (x * x)))


def _sigmoid(x):
    return 1.0 / (1.0 + jnp.exp(-x))


def _log1p(e):
    w = 1.0 + e
    return jnp.where(w == 1.0, e, e * jnp.log(w) / jnp.where(w == 1.0, 1.0, w - 1.0))


def _softplus(x):
    return jnp.maximum(x, 0.0) + _log1p(jnp.exp(-jnp.abs(x)))


def _sqrt_neg_expm1_2x(x):
    t = jnp.tanh(x)
    m = -2.0 * t
    return m * lax.rsqrt(jnp.maximum(m * (1.0 - t), F32_TINY))


def _mod_kernel(c_ref, w_ref, b_ref, o_ref):
    c = c_ref[...]
    sc = c * _sigmoid(c)
    o_ref[0] = jnp.dot(sc, w_ref[0], preferred_element_type=F32, precision=HIGHEST) + b_ref[0]


def _modulation(cond, w_mod, b_mod):
    L, D, D6 = w_mod.shape
    R = cond.shape[0]
    tn = D6 // 4
    return pl.pallas_call(
        _mod_kernel,
        grid=(L, D6 // tn),
        in_specs=[pl.BlockSpec((R, D), lambda l, n: (0, 0)),
                  pl.BlockSpec((1, D, tn), lambda l, n: (l, 0, n)),
                  pl.BlockSpec((1, 1, tn), lambda l, n: (l, 0, n))],
        out_specs=pl.BlockSpec((1, R, tn), lambda l, n: (l, 0, n)),
        out_shape=jax.ShapeDtypeStruct((L, R, D6), F32),
        compiler_params=_cparams(("arbitrary", "arbitrary")),
        name="modulation",
    )(cond, w_mod, b_mod.reshape(L, 1, D6))


def _add_kernel(x_ref, p_ref, o_ref):
    o_ref[0] = x_ref[0] + p_ref[...]


def _add_pos(x, pe):
    B, T, D = x.shape
    tt = min(T, 512)
    return pl.pallas_call(
        _add_kernel,
        grid=(T // tt, B),
        in_specs=[pl.BlockSpec((1, tt, D), lambda t, b: (b, t, 0)),
                  pl.BlockSpec((tt, D), lambda t, b: (t, 0))],
        out_specs=pl.BlockSpec((1, tt, D), lambda t, b: (b, t, 0)),
        out_shape=jax.ShapeDtypeStruct((B, T, D), F32),
        compiler_params=_cparams(("arbitrary", "arbitrary")),
        name="add_pos",
    )(x, pe)


def _lru_gates(xb, wg_ref, ba, bx, sp, a_scr, u_scr):
    heads = wg_ref.shape[0]
    for h in range(heads):
        hs = slice(h * LANE, (h + 1) * LANE)
        xh = xb[:, hs]
        z = jnp.dot(xh.astype(BF16), wg_ref[h], preferred_element_type=F32)
        r = _sigmoid(z[:, :LANE] + ba[:, hs])
        i = _sigmoid(z[:, LANE:] + bx[:, hs])
        log_a = (-LRU_C) * r * sp[:, hs]
        a = jnp.exp(log_a)
        u = _sqrt_neg_expm1_2x(log_a) * (i * xh)
        a_scr[:, hs] = a
        u_scr[:, hs] = u


def _lru_fwd_kernel(x_ref, xn_ref, mod_ref, g_ref, win_ref, cw_ref, cb_ref, wg_ref,
                    ba_ref, bx_ref, lam_ref, h0_ref,
                    gate_ref, xb_ref, hf_ref, fin_ref,
                    carry_ref, prev_scr, a_scr, u_scr, *, tt, n_t, width):
    t = pl.program_id(0)
    nb = x_ref.shape[0]
    rows = tt * nb

    @pl.when(t == 0)
    def _():
        carry_ref[...] = h0_ref[...]
        prev_scr[...] = jnp.zeros(prev_scr.shape, F32)

    ext = jnp.concatenate([jnp.swapaxes(x_ref[...], 0, 1), xn_ref[:, 0, :][None]], axis=0)
    hn = _rms(ext, g_ref[...]) * (1.0 + mod_ref[:, 1, :][None]) + mod_ref[:, 0, :][None]
    hn = hn.reshape(rows + nb, hn.shape[-1]).astype(BF16)
    proj = jnp.dot(hn, win_ref[...], preferred_element_type=F32)
    gate_ref[0] = _gelu_tanh(proj[:rows, :width])

    ahead = jnp.where(t < n_t - 1, proj[rows:, width:], 0.0)
    span = jnp.concatenate([prev_scr[...], proj[:rows, width:], ahead], axis=0)
    prev_scr[...] = proj[rows - 2 * nb:rows, width:]
    xb = cb_ref[...]
    for k in range(CONV_W):
        xb = xb + cw_ref[k:k + 1, :] * span[k * nb:k * nb + rows, :]
    xb_ref[0] = xb

    sp = _softplus(-lam_ref[...])
    _lru_gates(xb, wg_ref, ba_ref[...], bx_ref[...], sp, a_scr, u_scr)

    def body(ti, h):
        r0 = pl.multiple_of(ti * nb, nb)
        h = u_scr[pl.ds(r0, nb), :] + a_scr[pl.ds(r0, nb), :] * h
        hf_ref[0, pl.ds(r0, nb), :] = h
        return h

    h = lax.fori_loop(0, tt, body, carry_ref[...])
    carry_ref[...] = h
    fin_ref[...] = h


def _lru_bwd_kernel(x_ref, gate_ref, xb_ref, hf_ref, mod_ref, wg_ref, ba_ref, bx_ref, lam_ref, h0_ref,
                    wout_ref, g2_ref, r_ref,
                    xo_ref, hn_ref, lg_ref, fin_ref,
                    carry_ref, a_scr, u_scr, hb_scr, *, tt):
    t = pl.program_id(0)
    nb = x_ref.shape[0]

    @pl.when(t == 0)
    def _():
        carry_ref[...] = h0_ref[...]

    sp = _softplus(-lam_ref[...])
    _lru_gates(xb_ref[0], wg_ref, ba_ref[...], bx_ref[...], sp, a_scr, u_scr)

    def body(ti, h):
        r0 = pl.multiple_of((tt - 1 - ti) * nb, nb)
        h = u_scr[pl.ds(r0, nb), :] + a_scr[pl.ds(r0, nb), :] * h
        hb_scr[pl.ds(r0, nb), :] = h
        return h

    h = lax.fori_loop(0, tt, body, carry_ref[...])
    carry_ref[...] = h
    fin_ref[...] = h

    y = ((hf_ref[0] + hb_scr[...]) * gate_ref[0]).astype(BF16)
    y = jnp.dot(y, wout_ref[...], preferred_element_type=F32)
    y = jnp.swapaxes(y.reshape(tt, nb, y.shape[-1]), 0, 1)
    xo = x_ref[...] + mod_ref[:, 2:3, :] * y
    xo_ref[...] = xo
    hn = _rms(xo, g2_ref[...]) * (1.0 + mod_ref[:, 4:5, :]) + mod_ref[:, 3:4, :]
    hn_ref[...] = hn.astype(BF16)
    lg = _router_logits(hn.reshape(nb * tt, hn.shape[-1]), r_ref)
    lg_ref[...] = lg.reshape(nb, tt, lg.shape[-1])


def _router_logits(hn, r_ref):
    h_hi = hn.astype(BF16)
    h_lo = (hn - h_hi.astype(F32)).astype(BF16)
    return (jnp.dot(h_hi, r_ref[0], preferred_element_type=F32)
            + jnp.dot(h_hi, r_ref[1], preferred_element_type=F32)
            + jnp.dot(h_lo, r_ref[0], preferred_element_type=F32))


def _split_router(router):
    hi = router.astype(BF16)
    lo = (router - hi.astype(F32)).astype(BF16)
    return jnp.stack([hi, lo])


def _moe_prenorm(xo, mod_ref, g2_ref, r_ref, hn_ref, lg_ref):
    hn = _rms(xo, g2_ref[...]) * (1.0 + mod_ref[0, 4:5, :]) + mod_ref[0, 3:4, :]
    hn_ref[0] = hn.astype(BF16)
    lg_ref[0] = _router_logits(hn, r_ref)


def _lru_layer(x, mod, h0, p, j, norm1_g, norm2_g, router2):
    B, T, D = x.shape
    W = p["lru_lam"].shape[-1]
    H = W // LANE
    E = router2.shape[-1]
    tt = min(T, max(SUBLANE, LRU_ROWS // B))
    n_t = T // tt
    rows = tt * B
    bc = mod.shape[0]
    assert B % SUBLANE == 0 and T % tt == 0 and tt % SUBLANE == 0 and bc in (1, B)
    n8 = T // SUBLANE
    r8 = tt // SUBLANE

    def wg(d):
        return jnp.concatenate([p["lru_w_a"][j, d], p["lru_w_x"][j, d]], axis=-1).astype(BF16)

    def vec(name, d):
        return p[name][j, d].reshape(1, W)

    full = lambda shape: pl.BlockSpec(shape, lambda t: (0,) * len(shape), pipeline_mode=pl.Buffered(1))
    fwd_x = lambda t: (0, t, 0)
    fwd_w = lambda t: (t, 0, 0)
    tile_w = lambda imap: pl.BlockSpec((1, rows, W), imap)
    inner = jax.ShapeDtypeStruct((n_t, rows, W), F32)
    gate, xb, hf, fin_f = pl.pallas_call(
        functools.partial(_lru_fwd_kernel, tt=tt, n_t=n_t, width=W),
        grid=(n_t,),
        in_specs=[pl.BlockSpec((B, tt, D), fwd_x),
                  pl.BlockSpec((B, SUBLANE, D), lambda t: (0, jnp.minimum((t + 1) * r8, n8 - 1), 0)),
                  full((bc, 6, D)),
                  full((1, D)), full((D, 2 * W)), full((CONV_W, W)), full((1, W)),
                  full((H, LANE, 2 * LANE)), full((1, W)), full((1, W)), full((1, W)),
                  full((B, W))],
        out_specs=[tile_w(fwd_w), tile_w(fwd_w), tile_w(fwd_w), pl.BlockSpec((B, W), lambda t: (0, 0))],
        out_shape=[inner] * 3 + [jax.ShapeDtypeStruct((B, W), F32)],
        scratch_shapes=[pltpu.VMEM((B, W), F32), pltpu.VMEM((2 * B, W), F32),
                        pltpu.VMEM((rows, W), F32), pltpu.VMEM((rows, W), F32)],
        compiler_params=_cparams(("arbitrary",)),
        name="lru_fwd",
    )(x, x, mod, norm1_g.reshape(1, D), p["lru_w_in"][j].astype(BF16), p["lru_conv_w"][j],
      p["lru_conv_b"][j].reshape(1, W), wg(0), vec("lru_b_a", 0), vec("lru_b_x", 0), vec("lru_lam", 0),
      h0[:, 0])

    bwd_x = lambda t: (0, n_t - 1 - t, 0)
    bwd_w = lambda t: (n_t - 1 - t, 0, 0)
    xo, hn2, lg, fin_b = pl.pallas_call(
        functools.partial(_lru_bwd_kernel, tt=tt),
        grid=(n_t,),
        in_specs=[pl.BlockSpec((B, tt, D), bwd_x), tile_w(bwd_w), tile_w(bwd_w), tile_w(bwd_w),
                  full((bc, 6, D)),
                  full((H, LANE, 2 * LANE)), full((1, W)), full((1, W)), full((1, W)),
                  full((B, W)),
                  full((W, D)), full((1, D)), full((2, D, E))],
        out_specs=[pl.BlockSpec((B, tt, D), bwd_x), pl.BlockSpec((B, tt, D), bwd_x),
                   pl.BlockSpec((B, tt, E), bwd_x), pl.BlockSpec((B, W), lambda t: (0, 0))],
        out_shape=[jax.ShapeDtypeStruct((B, T, D), F32), jax.ShapeDtypeStruct((B, T, D), BF16),
                   jax.ShapeDtypeStruct((B, T, E), F32), jax.ShapeDtypeStruct((B, W), F32)],
        scratch_shapes=[pltpu.VMEM((B, W), F32), pltpu.VMEM((rows, W), F32), pltpu.VMEM((rows, W), F32),
                        pltpu.VMEM((rows, W), F32)],
        compiler_params=_cparams(("arbitrary",)),
        name="lru_bwd",
    )(x, gate, xb, hf, mod, wg(1), vec("lru_b_a", 1), vec("lru_b_x", 1), vec("lru_lam", 1), h0[:, 1],
      p["lru_w_out"][j].astype(BF16), norm2_g.reshape(1, D), router2)
    return xo, hn2, lg.reshape(B * T, E).T, jnp.stack([fin_f, fin_b], axis=1)


def _sgu_kernel(x_ref, mod_ref, g_ref, win_ref, ng_ref, ws_ref, bs_ref, wout_ref, g2_ref, rt_ref,
                xo_ref, hn_ref, lg_ref, u_scr, v_scr, p_scr, *, tt, sgw):
    x = x_ref[0]
    hn = (_rms(x, g_ref[...]) * (1.0 + mod_ref[0, 1:2, :]) + mod_ref[0, 0:1, :]).astype(BF16)
    u_scr[...] = _gelu_tanh(jnp.dot(hn, win_ref[:, :sgw], preferred_element_type=F32))
    v = _gelu_tanh(jnp.dot(hn, win_ref[:, sgw:], preferred_element_type=F32))
    v_scr[...] = _rms(v, ng_ref[...]).astype(BF16)
    gd = sgw // SG_GROUPS
    for n in range(tt // CHUNK):
        rs = slice(n * CHUNK, (n + 1) * CHUNK)
        for g in range(SG_GROUPS):
            cs = slice(g * gd, (g + 1) * gd)
            sv = jnp.dot(ws_ref[g], v_scr[rs, cs], preferred_element_type=F32) + bs_ref[:, g:g + 1]
            p_scr[rs, cs] = (u_scr[rs, cs] * sv).astype(BF16)
    y = jnp.dot(p_scr[...], wout_ref[...], preferred_element_type=F32)
    xo = x + mod_ref[0, 2:3, :] * y
    xo_ref[0] = xo
    _moe_prenorm(xo, mod_ref, g2_ref, rt_ref, hn_ref, lg_ref)


def _sgu_layer(x, mod, p, j, norm1_g, norm2_g, router2):
    B, T, D = x.shape
    sgw = p["sg_norm_g"].shape[-1]
    E = router2.shape[-1]
    tt = min(T, SGU_ROWS)
    n_t = T // tt
    bc = mod.shape[0]
    mod_map = (lambda b, t: (b, 0, 0)) if bc > 1 else (lambda b, t: (0, 0, 0))
    full = lambda shape: pl.BlockSpec(shape, lambda b, t: (0,) * len(shape), pipeline_mode=pl.Buffered(1))
    tile = pl.BlockSpec((1, tt, D), lambda b, t: (b, t, 0))
    xo, hn2, lg = pl.pallas_call(
        functools.partial(_sgu_kernel, tt=tt, sgw=sgw),
        grid=(B, n_t),
        in_specs=[tile, pl.BlockSpec((1, 6, D), mod_map), full((1, D)), full((D, 2 * sgw)), full((1, sgw)),
                  full((SG_GROUPS, CHUNK, CHUNK)), full((CHUNK, SG_GROUPS)), full((sgw, D)),
                  full((1, D)), full((2, D, E))],
        out_specs=[tile, tile, pl.BlockSpec((1, tt, E), lambda b, t: (b, t, 0))],
        out_shape=[jax.ShapeDtypeStruct((B, T, D), F32), jax.ShapeDtypeStruct((B, T, D), BF16),
                   jax.ShapeDtypeStruct((B, T, E), F32)],
        scratch_shapes=[pltpu.VMEM((tt, sgw), F32), pltpu.VMEM((tt, sgw), BF16), pltpu.VMEM((tt, sgw), BF16)],
        compiler_params=_cparams(("arbitrary", "arbitrary")),
        name="sgu",
    )(x, mod, norm1_g.reshape(1, D), p["sg_w_in"][j].astype(BF16), p["sg_norm_g"][j].reshape(1, sgw),
      p["sg_w_s"][j].astype(BF16), p["sg_b_s"][j].T, p["sg_w_out"][j].astype(BF16),
      norm2_g.reshape(1, D), router2)
    return xo, hn2, lg.reshape(B * T, E).T


def _select_kernel(lg_ref, pos_ref, g_ref, off_ref, aff_scr, *, cap, tb):
    E, N = lg_ref.shape
    n_tile = N // tb
    lg = lg_ref[...]
    ex = jnp.exp(lg - jnp.max(lg, axis=0, keepdims=True))
    aff_scr[...] = ex / jnp.sum(ex, axis=0, keepdims=True)

    def search(it, cur):
        cand = cur | (1 << (30 - it))
        bits = pltpu.bitcast(aff_scr[...], I32)
        cnt = jnp.sum(jnp.where(bits >= cand, 1.0, 0.0), axis=1, keepdims=True)
        return jnp.where(cnt >= cap, cand, cur)

    thr = lax.fori_loop(0, 31, search, jnp.zeros((E, 1), I32))
    bits = pltpu.bitcast(aff_scr[...], I32)
    n_gt = jnp.sum(jnp.where(bits > thr, 1.0, 0.0), axis=1, keepdims=True)
    need = cap - n_gt

    tri = (lax.broadcasted_iota(I32, (tb, tb), 0) <= lax.broadcasted_iota(I32, (tb, tb), 1)).astype(BF16)
    lane = lax.broadcasted_iota(I32, off_ref.shape, 1)

    def chunk(c, carry):
        c_eq, c_pos = carry
        l0 = pl.multiple_of(c * tb, tb)
        aff = aff_scr[:, pl.ds(l0, tb)]
        b = pltpu.bitcast(aff, I32)
        eq = b == thr
        eqf = jnp.where(eq, 1.0, 0.0)
        rank = jnp.dot(eqf.astype(BF16), tri, preferred_element_type=F32) - eqf + c_eq
        sel = (b > thr) | (eq & (rank < need))
        self_ = jnp.where(sel, 1.0, 0.0)
        inc = jnp.dot(self_.astype(BF16), tri, preferred_element_type=F32)
        pos = inc - self_ + c_pos
        pos_ref[:, pl.ds(l0, tb)] = jnp.where(sel, pos, -1.0).astype(I32)
        g_ref[:, pl.ds(l0, tb)] = jnp.where(sel, aff, 0.0)
        off_ref[...] = jnp.where(lane == c, jnp.broadcast_to(c_pos, off_ref.shape).astype(I32), off_ref[...])
        return (c_eq + jnp.sum(eqf, axis=1, keepdims=True), c_pos + inc[:, tb - 1:tb])

    off_ref[...] = jnp.zeros(off_ref.shape, I32)
    zero = jnp.zeros((E, 1), F32)
    _, total = lax.fori_loop(0, n_tile, chunk, (zero, zero))
    off_ref[...] = jnp.where(lane == n_tile, jnp.broadcast_to(total, off_ref.shape).astype(I32), off_ref[...])


def _select(logits_t, cap, tb):
    E, N = logits_t.shape
    offw = -(-(N // tb + 1) // LANE) * LANE
    return pl.pallas_call(
        functools.partial(_select_kernel, cap=cap, tb=tb),
        out_shape=[jax.ShapeDtypeStruct((E, N), I32), jax.ShapeDtypeStruct((E, N), F32),
                   jax.ShapeDtypeStruct((E, offw), I32)],
        scratch_shapes=[pltpu.VMEM((E, N), F32)],
        compiler_params=pltpu.CompilerParams(vmem_limit_bytes=VMEM_LIMIT),
        name="moe_select",
    )(logits_t)


def _expert_kernel(off_ref, x_ref, pos_ref, wg_ref, wu_ref, wd_ref, y_ref,
                   xs_scr, *, n_gather, n_blk, tb, sb, ws, offw):
    s = pl.program_id(0)
    d = pl.program_id(1)
    sub = x_ref.shape[1] // tb
    grp = xs_scr.shape[0]
    row_iota = lax.broadcasted_iota(I32, (ws, tb), 0)

    @pl.when(d == 0)
    def _():
        xs_scr[...] = jnp.zeros(xs_scr.shape, xs_scr.dtype)

    @pl.when(d < n_gather)
    def _gather():
        for j in range(sub):
            i = d * sub + j
            xt = x_ref[0, j * tb:(j + 1) * tb, :]
            lanes = slice(j * tb, (j + 1) * tb)
            bases, hits = [], []
            for k in range(grp):
                e = s * grp + k
                o0 = off_ref[e * offw + i]
                base = pl.multiple_of((o0 // BF16_ROWS) * BF16_ROWS, BF16_ROWS)
                bases.append(base)
                hits.append(pos_ref[k, 0, :, lanes] == (base + row_iota))
            stacked = jnp.where(jnp.concatenate(hits, axis=0), 1.0, 0.0).astype(BF16)
            rows = jnp.dot(stacked, xt, preferred_element_type=F32)
            for k in range(grp):
                xs_scr[k, pl.ds(bases[k], ws), :] += rows[k * ws:(k + 1) * ws, :].astype(BF16)
            for k in range(grp):
                e = s * grp + k
                end = off_ref[e * offw + i + 1]
                n_more = jnp.maximum(end - (bases[k] + ws) + ws - 1, 0) // ws

                def more(ch, _):
                    r0 = pl.multiple_of(bases[k] + (ch + 1) * ws, BF16_ROWS)
                    hit = pos_ref[k, 0, :, lanes] == (r0 + row_iota)
                    extra = jnp.dot(jnp.where(hit, 1.0, 0.0).astype(BF16), xt, preferred_element_type=F32)
                    xs_scr[k, pl.ds(r0, ws), :] += extra.astype(BF16)
                    return 0

                lax.fori_loop(0, n_more, more, 0)

    @pl.when(d >= n_gather)
    def _ffn():
        step = d - n_gather
        k = step // n_blk
        r0 = pl.multiple_of((step - k * n_blk) * sb, sb)
        xb = xs_scr[k, pl.ds(r0, sb), :]
        hg = jnp.dot(xb, wg_ref[0, 0], preferred_element_type=F32)
        hu = jnp.dot(xb, wu_ref[0, 0], preferred_element_type=F32)
        h = (hg * _sigmoid(hg) * hu).astype(BF16)
        o = jnp.dot(h, wd_ref[0, 0], preferred_element_type=F32)
        y_ref[...] = o.astype(BF16)


def _experts(hn2, pos, off, w_gate, w_up, w_down, layer, cap, tb, sb, ws, grp):
    N, D = hn2.shape
    E = pos.shape[0]
    F = w_gate.shape[-1]
    offw = off.shape[1]
    n_blk = cap // sb
    ns = min(N, GATHER_TOKENS)
    n_gather = N // ns
    assert ns % tb == 0 and cap % sb == 0 and N % ns == 0 and E % grp == 0
    tok = lambda s, d, off: jnp.minimum(d, n_gather - 1)
    ffn = lambda d: jnp.maximum(d - n_gather, 0)
    emap = lambda s, d, off: (s, tok(s, d, off), 0, 0)
    wmap = lambda s, d, off: (layer, s * grp + ffn(d) // n_blk, 0, 0)
    grid_spec = pltpu.PrefetchScalarGridSpec(
        num_scalar_prefetch=1,
        grid=(E // grp, n_gather + grp * n_blk),
        in_specs=[pl.BlockSpec((1, ns, D), lambda s, d, off: (tok(s, d, off), 0, 0)),
                  pl.BlockSpec((grp, 1, 1, ns), emap),
                  pl.BlockSpec((1, 1, D, F), wmap), pl.BlockSpec((1, 1, D, F), wmap),
                  pl.BlockSpec((1, 1, F, D), wmap)],
        out_specs=pl.BlockSpec((sb, D), lambda s, d, off: (s * grp * n_blk + ffn(d), 0)),
        scratch_shapes=[pltpu.VMEM((grp, cap + ws, D), BF16)],
    )
    return pl.pallas_call(
        functools.partial(_expert_kernel, n_gather=n_gather, n_blk=n_blk, tb=tb, sb=sb, ws=ws, offw=offw),
        grid_spec=grid_spec,
        out_shape=jax.ShapeDtypeStruct((E * cap, D), BF16),
        compiler_params=_cparams(("arbitrary", "arbitrary")),
        name="moe_experts",
    )(off.reshape(-1), hn2.reshape(n_gather, ns, D), pos.reshape(E, n_gather, 1, ns), w_gate, w_up, w_down)


def _window_base(off_ref, e, i, offw, cap, ws):
    o0 = off_ref[e * offw + i]
    return jnp.minimum((o0 // BF16_ROWS) * BF16_ROWS, cap - ws)


def _combine_kernel(off_ref, x_ref, pos_ref, g_ref, mod_ref, fg_ref, y_hbm, xo_ref, win, extra, sem, xsem,
                    *, n_exp, n_tile, cap, ws, offw, final_norm):
    i = pl.program_id(0)
    slot = i % 2
    tb = x_ref.shape[0]

    def window_copy(ti, sl, e):
        base = _window_base(off_ref, e, ti, offw, cap, ws)
        return pltpu.make_async_copy(y_hbm.at[pl.ds(e * cap + base, ws), :],
                                     win.at[sl, e], sem.at[sl, e])

    @pl.when(i == 0)
    def _():
        for e in range(n_exp):
            window_copy(0, 0, e).start()

    @pl.when(i + 1 < n_tile)
    def _():
        for e in range(n_exp):
            window_copy(i + 1, 1 - slot, e).start()

    half = ws // 2
    kw = n_exp * half
    gate_t = jnp.transpose(g_ref[...]).astype(BF16)
    shift = half.bit_length() - 1
    post = jnp.transpose(pos_ref[...].astype(F32)).astype(I32)
    digits = jnp.where(post < 0, NO_SLOT, post)
    col = lax.broadcasted_iota(I32, (n_exp, kw), 1)
    spread = (lax.shift_right_logical(col, shift) == lax.broadcasted_iota(I32, (n_exp, kw), 0)).astype(BF16)
    hi = lax.shift_right_logical(digits, DIGIT_BITS).astype(F32).astype(BF16)
    lo = (digits & ((1 << DIGIT_BITS) - 1)).astype(F32).astype(BF16)
    rel = (float(1 << DIGIT_BITS) * jnp.dot(hi, spread, preferred_element_type=F32)
           + jnp.dot(lo, spread, preferred_element_type=F32))
    col1 = lax.broadcasted_iota(I32, (1, kw), 1)
    owner = lax.shift_right_logical(col1, shift)
    tgt = col1 & (half - 1)
    second = False
    for e in range(n_exp):
        base = _window_base(off_ref, e, i, offw, cap, ws)
        tgt = jnp.where(owner == e, tgt + base, tgt)
        second = jnp.logical_or(second, off_ref[e * offw + i + 1] > base + half)
        window_copy(i, slot, e).wait()
    tgt = tgt.astype(F32)
    weight = jnp.dot(gate_t, spread, preferred_element_type=F32)
    gate2 = mod_ref[0, 5:6, :]
    d_model = x_ref.shape[1]
    hit = jnp.where(rel == tgt, weight, 0.0).astype(BF16)
    acc = jnp.dot(hit, win[slot, :, :half, :].reshape(kw, d_model), preferred_element_type=F32)
    xo_ref[...] = x_ref[...] + gate2 * acc

    @pl.when(second)
    def _():
        hit2 = jnp.where(rel == tgt + float(half), weight, 0.0).astype(BF16)
        xo_ref[...] += gate2 * jnp.dot(hit2, win[slot, :, half:, :].reshape(kw, d_model),
                                       preferred_element_type=F32)

    lane = lax.broadcasted_iota(I32, (tb, ws), 1)

    for e in range(n_exp):
        end = off_ref[e * offw + i + 1]
        base = _window_base(off_ref, e, i, offw, cap, ws)
        n_more = jnp.maximum(end - (base + ws) + ws - 1, 0) // ws

        def more(k, _):
            b2 = jnp.minimum(base + (k + 1) * ws, cap - ws)
            cp = pltpu.make_async_copy(y_hbm.at[pl.ds(e * cap + b2, ws), :], extra, xsem)
            cp.start()
            cp.wait()
            pc = jnp.transpose(pos_ref[...].astype(F32)).astype(I32)[:, e:e + 1]
            h2 = (pc == (b2 + lane)) & (pc >= base + (k + 1) * ws)
            wcol = jnp.transpose(g_ref[...]).astype(BF16).astype(F32)[:, e:e + 1]
            xo_ref[...] += gate2 * jnp.dot(jnp.where(h2, wcol, 0.0).astype(BF16), extra[...],
                                           preferred_element_type=F32)
            return 0

        lax.fori_loop(0, n_more, more, 0)

    if final_norm:
        xo_ref[...] = _rms(xo_ref[...], fg_ref[...])


def _combine(x2, pos, g, off, y2, mod, tokens_per_batch, cap, tb, ws, final_g):
    N, D = x2.shape
    E = pos.shape[0]
    offw = off.shape[1]
    n_tile = N // tb
    per_b = tokens_per_batch // tb
    bc = mod.shape[0]
    mod_map = (lambda i, off: (i // per_b, 0, 0)) if bc > 1 else (lambda i, off: (0, 0, 0))
    grid_spec = pltpu.PrefetchScalarGridSpec(
        num_scalar_prefetch=1,
        grid=(n_tile,),
        in_specs=[pl.BlockSpec((tb, D), lambda i, off: (i, 0)),
                  pl.BlockSpec((E, tb), lambda i, off: (0, i)),
                  pl.BlockSpec((E, tb), lambda i, off: (0, i)),
                  pl.BlockSpec((1, 6, D), mod_map),
                  pl.BlockSpec((1, D), lambda i, off: (0, 0)),
                  pl.BlockSpec(memory_space=pl.ANY)],
        out_specs=pl.BlockSpec((tb, D), lambda i, off: (i, 0)),
        scratch_shapes=[pltpu.VMEM((2, E, ws, D), BF16), pltpu.VMEM((ws, D), BF16),
                        pltpu.SemaphoreType.DMA((2, E)), pltpu.SemaphoreType.DMA(())],
    )
    return pl.pallas_call(
        functools.partial(_combine_kernel, n_exp=E, n_tile=n_tile, cap=cap, ws=ws, offw=offw,
                          final_norm=final_g is not None),
        grid_spec=grid_spec,
        out_shape=jax.ShapeDtypeStruct((N, D), F32),
        compiler_params=_cparams(("arbitrary",)),
        name="moe_combine",
    )(off.reshape(-1), x2, pos, g, mod, (mod[0, 0] if final_g is None else final_g).reshape(1, D), y2)


def _moe(x, hn2, logits_t, mod, p, layer, final_g):
    B, T, D = x.shape
    N = B * T
    E = logits_t.shape[0]
    cap = EC_FACTOR * N // E
    tb = min(TOKEN_TILE, T)
    ws = min(SLOT_WINDOW, cap)
    per_expert = (cap + ws) * D * 2
    grp = 1
    while grp * 2 <= E and grp * 2 * per_expert <= STAGING_BYTES:
        grp *= 2
    sb = min(SLOT_BLOCK, cap)
    pos, g, off = _select(logits_t, cap, tb)
    y2 = _experts(hn2.reshape(N, D), pos, off, p["moe_w_gate"], p["moe_w_up"], p["moe_w_down"],
                  layer, cap, tb, sb, ws, grp)
    assert cap <= NO_SLOT // 2
    xo = _combine(x.reshape(N, D), pos, g, off, y2, mod, T, cap, tb, min(COMBINE_WINDOW, cap), final_g)
    return xo.reshape(B, T, D)


def _grid_pos_embed(n_tokens, d_model):
    rows = n_tokens // GRID_W
    row = jnp.repeat(jnp.arange(rows, dtype=F32), GRID_W)
    col = jnp.tile(jnp.arange(GRID_W, dtype=F32), rows)
    q = d_model // 4
    freq = jnp.exp(-math.log(POS_BASE) * jnp.arange(q, dtype=F32) / q)
    ang_r = row[:, None] * freq
    ang_c = col[:, None] * freq
    return jnp.concatenate([jnp.sin(ang_r), jnp.cos(ang_r), jnp.sin(ang_c), jnp.cos(ang_c)], axis=-1)


def _trunk(x, mods, h0, p):
    depth = p["norm1_g"].shape[0]
    finals = []
    for l in range(depth):
        mod = mods[l]
        router2 = _split_router(p["moe_router"][l])
        j = l // 2
        if l % 2 == 0:
            x, hn2, lg, fin = _lru_layer(x, mod, h0[:, j], p, j, p["norm1_g"][l], p["norm2_g"][l], router2)
            finals.append(fin)
        else:
            x, hn2, lg = _sgu_layer(x, mod, p, j, p["norm1_g"][l], p["norm2_g"][l], router2)
        x = _moe(x, hn2, lg, mod, p, l, p["final_norm_g"] if l == depth - 1 else None)
    return x, jnp.stack(finals, axis=1)


def kernel(x_prompt, x_sample, state_lru, c, c_ctx, norm1_g, norm2_g, w_mod, b_mod,
           lru_w_in, lru_conv_w, lru_conv_b, lru_w_a, lru_b_a, lru_w_x, lru_b_x, lru_lam, lru_w_out,
           sg_w_in, sg_norm_g, sg_w_s, sg_b_s, sg_w_out,
           moe_router, moe_w_gate, moe_w_up, moe_w_down, final_norm_g):
    p = dict(norm1_g=norm1_g, norm2_g=norm2_g, lru_w_in=lru_w_in, lru_conv_w=lru_conv_w,
             lru_conv_b=lru_conv_b, lru_w_a=lru_w_a, lru_b_a=lru_b_a, lru_w_x=lru_w_x, lru_b_x=lru_b_x,
             lru_lam=lru_lam, lru_w_out=lru_w_out, sg_w_in=sg_w_in, sg_norm_g=sg_norm_g, sg_w_s=sg_w_s,
             sg_b_s=sg_b_s, sg_w_out=sg_w_out, moe_router=moe_router, moe_w_gate=moe_w_gate.astype(BF16),
             moe_w_up=moe_w_up.astype(BF16), moe_w_down=moe_w_down.astype(BF16), final_norm_g=final_norm_g)
    L, D, _ = w_mod.shape
    bs = c.shape[0]
    n_lru, _, W = lru_lam.shape

    rows = -(-(1 + bs) // SUBLANE) * SUBLANE
    cond = jnp.zeros((rows, D), F32).at[0].set(c_ctx).at[1:1 + bs].set(c)
    mods = _modulation(cond, w_mod, b_mod).reshape(L, rows, 6, D)

    h0_ctx = jnp.zeros((x_prompt.shape[0], n_lru, 2, W), F32)
    y_prompt, new_state = _trunk(x_prompt, mods[:, 0:1], h0_ctx, p)

    xs = _add_pos(x_sample, _grid_pos_embed(x_sample.shape[1], D))
    y_sample, _ = _trunk(xs, mods[:, 1:1 + bs], state_lru, p)
    return (y_prompt, y_sample, new_state)
```

```python
import functools
import math

import jax
import jax.numpy as jnp
from jax import lax
from jax.experimental import pallas as pl
from jax.experimental.pallas import tpu as pltpu

F32 = jnp.float32
BF16 = jnp.bfloat16
I32 = jnp.int32
HIGHEST = lax.Precision.HIGHEST

RMS_EPS = 1e-6
F32_TINY = 1e-30
LRU_C = 8.0
CONV_W = 4
CHUNK = 128
SG_GROUPS = 8
GRID_W = 64
POS_BASE = 10000.0
EC_FACTOR = 2
LANE = 128
SUBLANE = 8
BF16_ROWS = 16
VMEM_LIMIT = 56 * 1024 * 1024

TOKEN_TILE = 256
SLOT_BLOCK = 1024
GATHER_TOKENS = 2048
LRU_ROWS = 512
SGU_ROWS = 512
STAGING_BYTES = 22 * 1024 * 1024
COMBINE_WINDOW = 128
DIGIT_BITS = 6
NO_SLOT = 1 << 14
SLOT_WINDOW = 64


def _cparams(sem):
    return pltpu.CompilerParams(dimension_semantics=sem, vmem_limit_bytes=VMEM_LIMIT)


def _rms(x, g):
    return x * lax.rsqrt(jnp.mean(x * x, axis=-1, keepdims=True) + RMS_EPS) * g


def _gelu_tanh(x):
    c = math.sqrt(2.0 / math.pi)
    hx = 0.5 * x
    return hx + hx * jnp.tanh(x * (c + (c * 0.044715) * (x * x)))


def _sigmoid(x):
    return 1.0 / (1.0 + jnp.exp(-x))


def _log1p(e):
    w = 1.0 + e
    return jnp.where(w == 1.0, e, e * jnp.log(w) / jnp.where(w == 1.0, 1.0, w - 1.0))


def _softplus(x):
    return jnp.maximum(x, 0.0) + _log1p(jnp.exp(-jnp.abs(x)))


def _sqrt_neg_expm1_2x(x):
    t = jnp.tanh(x)
    m = -2.0 * t
    return m * lax.rsqrt(jnp.maximum(m * (1.0 - t), F32_TINY))


def _mod_kernel(c_ref, w_ref, b_ref, o_ref):
    c = c_ref[...]
    sc = c * _sigmoid(c)
    o_ref[0] = jnp.dot(sc, w_ref[0], preferred_element_type=F32, precision=HIGHEST) + b_ref[0]


def _modulation(cond, w_mod, b_mod):
    L, D, D6 = w_mod.shape
    R = cond.shape[0]
    tn = D6 // 4
    return pl.pallas_call(
        _mod_kernel,
        grid=(L, D6 // tn),
        in_specs=[pl.BlockSpec((R, D), lambda l, n: (0, 0)),
                  pl.BlockSpec((1, D, tn), lambda l, n: (l, 0, n)),
                  pl.BlockSpec((1, 1, tn), lambda l, n: (l, 0, n))],
        out_specs=pl.BlockSpec((1, R, tn), lambda l, n: (l, 0, n)),
        out_shape=jax.ShapeDtypeStruct((L, R, D6), F32),
        compiler_params=_cparams(("arbitrary", "arbitrary")),
        name="modulation",
    )(cond, w_mod, b_mod.reshape(L, 1, D6))


def _lru_gates(xb, wg_ref, ba, bx, sp, a_scr, u_scr):
    heads = wg_ref.shape[0]
    for h in range(heads):
        hs = slice(h * LANE, (h + 1) * LANE)
        xh = xb[:, hs]
        z = jnp.dot(xh.astype(BF16), wg_ref[h], preferred_element_type=F32)
        tr = jnp.tanh(z[:, :LANE] + ba[:, hs])
        ti = jnp.tanh(z[:, LANE:] + bx[:, hs])
        log_a = sp[:, hs] + sp[:, hs] * tr
        a = jnp.exp(log_a)
        hx = 0.5 * xh
        u = _sqrt_neg_expm1_2x(log_a) * (hx + hx * ti)
        a_scr[:, hs] = a
        u_scr[:, hs] = u


def _lru_fwd_kernel(x_ref, xn_ref, pe_ref, pen_ref, mod_ref, g_ref, win_ref, cw_ref, cb_ref, wg_ref,
                    ba_ref, bx_ref, lam_ref, h0_ref,
                    gate_ref, xb_ref, hf_ref, fin_ref,
                    carry_ref, prev_scr, a_scr, u_scr, *, tt, n_t, width):
    t = pl.program_id(0)
    nb = x_ref.shape[0]
    rows = tt * nb

    @pl.when(t == 0)
    def _():
        carry_ref[...] = h0_ref[...]
        prev_scr[...] = jnp.zeros(prev_scr.shape, F32)

    ext = jnp.concatenate([jnp.swapaxes(x_ref[...], 0, 1) + pe_ref[...][:, None, :],
                           (xn_ref[:, 0, :] + pen_ref[0:1, :])[None]], axis=0)
    hn = _rms(ext, g_ref[...]) * (1.0 + mod_ref[:, 1, :][None]) + mod_ref[:, 0, :][None]
    hn = hn.reshape(rows + nb, hn.shape[-1]).astype(BF16)
    proj = jnp.dot(hn, win_ref[...], preferred_element_type=F32)
    gate_ref[0] = _gelu_tanh(proj[:rows, :width])

    ahead = jnp.where(t < n_t - 1, proj[rows:, width:], 0.0)
    span = jnp.concatenate([prev_scr[...], proj[:rows, width:], ahead], axis=0)
    prev_scr[...] = proj[rows - 2 * nb:rows, width:]
    xb = cb_ref[...]
    for k in range(CONV_W):
        xb = xb + cw_ref[k:k + 1, :] * span[k * nb:k * nb + rows, :]
    xb_ref[0] = xb

    sp = (-0.5 * LRU_C) * _softplus(-lam_ref[...])
    _lru_gates(xb, wg_ref, ba_ref[...], bx_ref[...], sp, a_scr, u_scr)

    def body(ti, h):
        r0 = pl.multiple_of(ti * nb, nb)
        h = u_scr[pl.ds(r0, nb), :] + a_scr[pl.ds(r0, nb), :] * h
        hf_ref[0, pl.ds(r0, nb), :] = h
        return h

    h = lax.fori_loop(0, tt, body, carry_ref[...])
    carry_ref[...] = h
    fin_ref[...] = h


def _lru_bwd_kernel(x_ref, pe_ref, gate_ref, xb_ref, hf_ref, mod_ref, wg_ref, ba_ref, bx_ref, lam_ref, h0_ref,
                    wout_ref, g2_ref, r_ref,
                    xo_ref, hn_ref, lg_ref, fin_ref,
                    carry_ref, a_scr, u_scr, hb_scr, *, tt):
    t = pl.program_id(0)
    nb = x_ref.shape[0]

    @pl.when(t == 0)
    def _():
        carry_ref[...] = h0_ref[...]

    sp = (-0.5 * LRU_C) * _softplus(-lam_ref[...])
    _lru_gates(xb_ref[0], wg_ref, ba_ref[...], bx_ref[...], sp, a_scr, u_scr)

    def body(ti, h):
        r0 = pl.multiple_of((tt - 1 - ti) * nb, nb)
        h = u_scr[pl.ds(r0, nb), :] + a_scr[pl.ds(r0, nb), :] * h
        hb_scr[pl.ds(r0, nb), :] = h
        return h

    h = lax.fori_loop(0, tt, body, carry_ref[...])
    carry_ref[...] = h
    fin_ref[...] = h

    y = ((hf_ref[0] + hb_scr[...]) * gate_ref[0]).astype(BF16)
    y = jnp.dot(y, wout_ref[...], preferred_element_type=F32)
    y = jnp.swapaxes(y.reshape(tt, nb, y.shape[-1]), 0, 1)
    xo = (x_ref[...] + pe_ref[...][None]) + mod_ref[:, 2:3, :] * y
    xo_ref[...] = xo
    hn = _rms(xo, g2_ref[...]) * (1.0 + mod_ref[:, 4:5, :]) + mod_ref[:, 3:4, :]
    hn_ref[...] = hn.astype(BF16)
    lg = _router_logits(hn.reshape(nb * tt, hn.shape[-1]), r_ref)
    lg_ref[...] = lg.reshape(nb, tt, lg.shape[-1])


def _router_logits(hn, r_ref):
    h_hi = hn.astype(BF16)
    h_lo = (hn - h_hi.astype(F32)).astype(BF16)
    return (jnp.dot(h_hi, r_ref[0], preferred_element_type=F32)
            + jnp.dot(h_hi, r_ref[1], preferred_element_type=F32)
            + jnp.dot(h_lo, r_ref[0], preferred_element_type=F32))


def _split_router(router):
    hi = router.astype(BF16)
    lo = (router - hi.astype(F32)).astype(BF16)
    return jnp.stack([hi, lo])


def _moe_prenorm(xo, mod_ref, g2_ref, r_ref, hn_ref, lg_ref):
    hn = _rms(xo, g2_ref[...]) * (1.0 + mod_ref[0, 4:5, :]) + mod_ref[0, 3:4, :]
    hn_ref[0] = hn.astype(BF16)
    lg_ref[0] = _router_logits(hn, r_ref)


def _lru_layer(x, pe, mod, h0, p, j, norm1_g, norm2_g, router2):
    B, T, D = x.shape
    W = p["lru_lam"].shape[-1]
    H = W // LANE
    E = router2.shape[-1]
    tt = min(T, max(SUBLANE, LRU_ROWS // B))
    n_t = T // tt
    rows = tt * B
    bc = mod.shape[0]
    assert B % SUBLANE == 0 and T % tt == 0 and tt % SUBLANE == 0 and bc in (1, B)
    n8 = T // SUBLANE
    r8 = tt // SUBLANE

    def wg(d):
        return (0.5 * jnp.concatenate([p["lru_w_a"][j, d], p["lru_w_x"][j, d]], axis=-1)).astype(BF16)

    def vec(name, d, scale=1.0):
        return (scale * p[name][j, d]).reshape(1, W)

    full = lambda shape: pl.BlockSpec(shape, lambda t: (0,) * len(shape), pipeline_mode=pl.Buffered(1))
    fwd_x = lambda t: (0, t, 0)
    fwd_w = lambda t: (t, 0, 0)
    tile_w = lambda imap: pl.BlockSpec((1, rows, W), imap)
    inner = jax.ShapeDtypeStruct((n_t, rows, W), F32)
    gate, xb, hf, fin_f = pl.pallas_call(
        functools.partial(_lru_fwd_kernel, tt=tt, n_t=n_t, width=W),
        grid=(n_t,),
        in_specs=[pl.BlockSpec((B, tt, D), fwd_x),
                  pl.BlockSpec((B, SUBLANE, D), lambda t: (0, jnp.minimum((t + 1) * r8, n8 - 1), 0)),
                  pl.BlockSpec((tt, D), lambda t: (t, 0)),
                  pl.BlockSpec((SUBLANE, D), lambda t: (jnp.minimum((t + 1) * r8, n8 - 1), 0)),
                  full((bc, 6, D)),
                  full((1, D)), full((D, 2 * W)), full((CONV_W, W)), full((1, W)),
                  full((H, LANE, 2 * LANE)), full((1, W)), full((1, W)), full((1, W)),
                  full((B, W))],
        out_specs=[tile_w(fwd_w), tile_w(fwd_w), tile_w(fwd_w), pl.BlockSpec((B, W), lambda t: (0, 0))],
        out_shape=[inner] * 3 + [jax.ShapeDtypeStruct((B, W), F32)],
        scratch_shapes=[pltpu.VMEM((B, W), F32), pltpu.VMEM((2 * B, W), F32),
                        pltpu.VMEM((rows, W), F32), pltpu.VMEM((rows, W), F32)],
        compiler_params=_cparams(("arbitrary",)),
        name="lru_fwd",
    )(x, x, pe, pe, mod, norm1_g.reshape(1, D), p["lru_w_in"][j].astype(BF16), p["lru_conv_w"][j],
      p["lru_conv_b"][j].reshape(1, W), wg(0), vec("lru_b_a", 0, 0.5), vec("lru_b_x", 0, 0.5), vec("lru_lam", 0),
      h0[:, 0])

    bwd_x = lambda t: (0, n_t - 1 - t, 0)
    bwd_w = lambda t: (n_t - 1 - t, 0, 0)
    xo, hn2, lg, fin_b = pl.pallas_call(
        functools.partial(_lru_bwd_kernel, tt=tt),
        grid=(n_t,),
        in_specs=[pl.BlockSpec((B, tt, D), bwd_x), pl.BlockSpec((tt, D), lambda t: (n_t - 1 - t, 0)),
                  tile_w(bwd_w), tile_w(bwd_w), tile_w(bwd_w),
                  full((bc, 6, D)),
                  full((H, LANE, 2 * LANE)), full((1, W)), full((1, W)), full((1, W)),
                  full((B, W)),
                  full((W, D)), full((1, D)), full((2, D, E))],
        out_specs=[pl.BlockSpec((B, tt, D), bwd_x), pl.BlockSpec((B, tt, D), bwd_x),
                   pl.BlockSpec((B, tt, E), bwd_x), pl.BlockSpec((B, W), lambda t: (0, 0))],
        out_shape=[jax.ShapeDtypeStruct((B, T, D), F32), jax.ShapeDtypeStruct((B, T, D), BF16),
                   jax.ShapeDtypeStruct((B, T, E), F32), jax.ShapeDtypeStruct((B, W), F32)],
        scratch_shapes=[pltpu.VMEM((B, W), F32), pltpu.VMEM((rows, W), F32), pltpu.VMEM((rows, W), F32),
                        pltpu.VMEM((rows, W), F32)],
        compiler_params=_cparams(("arbitrary",)),
        name="lru_bwd",
    )(x, pe, gate, xb, hf, mod, wg(1), vec("lru_b_a", 1, 0.5), vec("lru_b_x", 1, 0.5), vec("lru_lam", 1), h0[:, 1],
      p["lru_w_out"][j].astype(BF16), norm2_g.reshape(1, D), router2)
    return xo, hn2, lg.reshape(B * T, E).T, jnp.stack([fin_f, fin_b], axis=1)


def _sgu_kernel(x_ref, mod_ref, g_ref, win_ref, ng_ref, ws_ref, bs_ref, wout_ref, g2_ref, rt_ref,
                xo_ref, hn_ref, lg_ref, u_scr, v_scr, p_scr, *, tt, sgw):
    x = x_ref[0]
    hn = (_rms(x, g_ref[...]) * (1.0 + mod_ref[0, 1:2, :]) + mod_ref[0, 0:1, :]).astype(BF16)
    u_scr[...] = _gelu_tanh(jnp.dot(hn, win_ref[:, :sgw], preferred_element_type=F32))
    v = _gelu_tanh(jnp.dot(hn, win_ref[:, sgw:], preferred_element_type=F32))
    v_scr[...] = _rms(v, ng_ref[...]).astype(BF16)
    gd = sgw // SG_GROUPS
    for n in range(tt // CHUNK):
        rs = slice(n * CHUNK, (n + 1) * CHUNK)
        for g in range(SG_GROUPS):
            cs = slice(g * gd, (g + 1) * gd)
            sv = jnp.dot(ws_ref[g], v_scr[rs, cs], preferred_element_type=F32) + bs_ref[:, g:g + 1]
            p_scr[rs, cs] = (u_scr[rs, cs] * sv).astype(BF16)
    y = jnp.dot(p_scr[...], wout_ref[...], preferred_element_type=F32)
    xo = x + mod_ref[0, 2:3, :] * y
    xo_ref[0] = xo
    _moe_prenorm(xo, mod_ref, g2_ref, rt_ref, hn_ref, lg_ref)


def _sgu_layer(x, mod, p, j, norm1_g, norm2_g, router2):
    B, T, D = x.shape
    sgw = p["sg_norm_g"].shape[-1]
    E = router2.shape[-1]
    tt = min(T, SGU_ROWS)
    n_t = T // tt
    bc = mod.shape[0]
    mod_map = (lambda b, t: (b, 0, 0)) if bc > 1 else (lambda b, t: (0, 0, 0))
    full = lambda shape: pl.BlockSpec(shape, lambda b, t: (0,) * len(shape), pipeline_mode=pl.Buffered(1))
    tile = pl.BlockSpec((1, tt, D), lambda b, t: (b, t, 0))
    xo, hn2, lg = pl.pallas_call(
        functools.partial(_sgu_kernel, tt=tt, sgw=sgw),
        grid=(B, n_t),
        in_specs=[tile, pl.BlockSpec((1, 6, D), mod_map), full((1, D)), full((D, 2 * sgw)), full((1, sgw)),
                  full((SG_GROUPS, CHUNK, CHUNK)), full((CHUNK, SG_GROUPS)), full((sgw, D)),
                  full((1, D)), full((2, D, E))],
        out_specs=[tile, tile, pl.BlockSpec((1, tt, E), lambda b, t: (b, t, 0))],
        out_shape=[jax.ShapeDtypeStruct((B, T, D), F32), jax.ShapeDtypeStruct((B, T, D), BF16),
                   jax.ShapeDtypeStruct((B, T, E), F32)],
        scratch_shapes=[pltpu.VMEM((tt, sgw), F32), pltpu.VMEM((tt, sgw), BF16), pltpu.VMEM((tt, sgw), BF16)],
        compiler_params=_cparams(("arbitrary", "arbitrary")),
        name="sgu",
    )(x, mod, norm1_g.reshape(1, D), p["sg_w_in"][j].astype(BF16), p["sg_norm_g"][j].reshape(1, sgw),
      p["sg_w_s"][j].astype(BF16), p["sg_b_s"][j].T, p["sg_w_out"][j].astype(BF16),
      norm2_g.reshape(1, D), router2)
    return xo, hn2, lg.reshape(B * T, E).T


def _select_kernel(lg_ref, pos_ref, g_ref, off_ref, aff_scr, *, cap, tb):
    E, N = lg_ref.shape
    n_tile = N // tb
    lg = lg_ref[...]
    ex = jnp.exp(lg - jnp.max(lg, axis=0, keepdims=True))
    aff_scr[...] = ex / jnp.sum(ex, axis=0, keepdims=True)

    def search(it, cur):
        cand = cur | (1 << (30 - it))
        bits = pltpu.bitcast(aff_scr[...], I32)
        cnt = jnp.sum(jnp.where(bits >= cand, 1.0, 0.0), axis=1, keepdims=True)
        return jnp.where(cnt >= cap, cand, cur)

    thr = lax.fori_loop(0, 31, search, jnp.zeros((E, 1), I32))
    bits = pltpu.bitcast(aff_scr[...], I32)
    n_gt = jnp.sum(jnp.where(bits > thr, 1.0, 0.0), axis=1, keepdims=True)
    need = cap - n_gt

    tri = (lax.broadcasted_iota(I32, (tb, tb), 0) <= lax.broadcasted_iota(I32, (tb, tb), 1)).astype(BF16)
    lane = lax.broadcasted_iota(I32, off_ref.shape, 1)

    def chunk(c, carry):
        c_eq, c_pos = carry
        l0 = pl.multiple_of(c * tb, tb)
        aff = aff_scr[:, pl.ds(l0, tb)]
        b = pltpu.bitcast(aff, I32)
        eq = b == thr
        eqf = jnp.where(eq, 1.0, 0.0)
        rank = jnp.dot(eqf.astype(BF16), tri, preferred_element_type=F32) - eqf + c_eq
        sel = (b > thr) | (eq & (rank < need))
        self_ = jnp.where(sel, 1.0, 0.0)
        inc = jnp.dot(self_.astype(BF16), tri, preferred_element_type=F32)
        pos = inc - self_ + c_pos
        pos_ref[:, pl.ds(l0, tb)] = jnp.where(sel, pos, -1.0).astype(I32)
        g_ref[:, pl.ds(l0, tb)] = jnp.where(sel, aff, 0.0)
        off_ref[...] = jnp.where(lane == c, jnp.broadcast_to(c_pos, off_ref.shape).astype(I32), off_ref[...])
        return (c_eq + jnp.sum(eqf, axis=1, keepdims=True), c_pos + inc[:, tb - 1:tb])

    off_ref[...] = jnp.zeros(off_ref.shape, I32)
    zero = jnp.zeros((E, 1), F32)
    _, total = lax.fori_loop(0, n_tile, chunk, (zero, zero))
    off_ref[...] = jnp.where(lane == n_tile, jnp.broadcast_to(total, off_ref.shape).astype(I32), off_ref[...])


def _select(logits_t, cap, tb):
    E, N = logits_t.shape
    offw = -(-(N // tb + 1) // LANE) * LANE
    return pl.pallas_call(
        functools.partial(_select_kernel, cap=cap, tb=tb),
        out_shape=[jax.ShapeDtypeStruct((E, N), I32), jax.ShapeDtypeStruct((E, N), F32),
                   jax.ShapeDtypeStruct((E, offw), I32)],
        scratch_shapes=[pltpu.VMEM((E, N), F32)],
        compiler_params=pltpu.CompilerParams(vmem_limit_bytes=VMEM_LIMIT),
        name="moe_select",
    )(logits_t)


def _expert_kernel(off_ref, x_ref, pos_ref, wg_ref, wu_ref, wd_ref, y_ref,
                   xs_scr, *, n_gather, n_blk, tb, sb, ws, offw):
    s = pl.program_id(0)
    d = pl.program_id(1)
    sub = x_ref.shape[1] // tb
    grp = xs_scr.shape[0]
    row_iota = lax.broadcasted_iota(I32, (ws, tb), 0)

    @pl.when(d == 0)
    def _():
        xs_scr[...] = jnp.zeros(xs_scr.shape, xs_scr.dtype)

    @pl.when(d < n_gather)
    def _gather():
        for j in range(sub):
            i = d * sub + j
            xt = x_ref[0, j * tb:(j + 1) * tb, :]
            lanes = slice(j * tb, (j + 1) * tb)
            bases, hits = [], []
            for k in range(grp):
                e = s * grp + k
                o0 = off_ref[e * offw + i]
                base = pl.multiple_of((o0 // BF16_ROWS) * BF16_ROWS, BF16_ROWS)
                bases.append(base)
                hits.append(pos_ref[k, 0, :, lanes] == (base + row_iota))
            stacked = jnp.where(jnp.concatenate(hits, axis=0), 1.0, 0.0).astype(BF16)
            rows = jnp.dot(stacked, xt, preferred_element_type=F32)
            for k in range(grp):
                xs_scr[k, pl.ds(bases[k], ws), :] += rows[k * ws:(k + 1) * ws, :].astype(BF16)
            for k in range(grp):
                e = s * grp + k
                end = off_ref[e * offw + i + 1]
                n_more = jnp.maximum(end - (bases[k] + ws) + ws - 1, 0) // ws

                def more(ch, _):
                    r0 = pl.multiple_of(bases[k] + (ch + 1) * ws, BF16_ROWS)
                    hit = pos_ref[k, 0, :, lanes] == (r0 + row_iota)
                    extra = jnp.dot(jnp.where(hit, 1.0, 0.0).astype(BF16), xt, preferred_element_type=F32)
                    xs_scr[k, pl.ds(r0, ws), :] += extra.astype(BF16)
                    return 0

                lax.fori_loop(0, n_more, more, 0)

    @pl.when(d >= n_gather)
    def _ffn():
        step = d - n_gather
        k = step // n_blk
        r0 = pl.multiple_of((step - k * n_blk) * sb, sb)
        xb = xs_scr[k, pl.ds(r0, sb), :]
        hg = jnp.dot(xb, wg_ref[0, 0], preferred_element_type=F32)
        hu = jnp.dot(xb, wu_ref[0, 0], preferred_element_type=F32)
        h = (hg * _sigmoid(hg) * hu).astype(BF16)
        o = jnp.dot(h, wd_ref[0, 0], preferred_element_type=F32)
        y_ref[...] = o.astype(BF16)


def _experts(hn2, pos, off, w_gate, w_up, w_down, layer, cap, tb, sb, ws, grp):
    N, D = hn2.shape
    E = pos.shape[0]
    F = w_gate.shape[-1]
    offw = off.shape[1]
    n_blk = cap // sb
    ns = min(N, GATHER_TOKENS)
    n_gather = N // ns
    assert ns % tb == 0 and cap % sb == 0 and N % ns == 0 and E % grp == 0
    tok = lambda s, d, off: jnp.minimum(d, n_gather - 1)
    ffn = lambda d: jnp.maximum(d - n_gather, 0)
    emap = lambda s, d, off: (s, tok(s, d, off), 0, 0)
    wmap = lambda s, d, off: (layer, s * grp + ffn(d) // n_blk, 0, 0)
    grid_spec = pltpu.PrefetchScalarGridSpec(
        num_scalar_prefetch=1,
        grid=(E // grp, n_gather + grp * n_blk),
        in_specs=[pl.BlockSpec((1, ns, D), lambda s, d, off: (tok(s, d, off), 0, 0)),
                  pl.BlockSpec((grp, 1, 1, ns), emap),
                  pl.BlockSpec((1, 1, D, F), wmap), pl.BlockSpec((1, 1, D, F), wmap),
                  pl.BlockSpec((1, 1, F, D), wmap)],
        out_specs=pl.BlockSpec((sb, D), lambda s, d, off: (s * grp * n_blk + ffn(d), 0)),
        scratch_shapes=[pltpu.VMEM((grp, cap + ws, D), BF16)],
    )
    return pl.pallas_call(
        functools.partial(_expert_kernel, n_gather=n_gather, n_blk=n_blk, tb=tb, sb=sb, ws=ws, offw=offw),
        grid_spec=grid_spec,
        out_shape=jax.ShapeDtypeStruct((E * cap, D), BF16),
        compiler_params=_cparams(("arbitrary", "arbitrary")),
        name="moe_experts",
    )(off.reshape(-1), hn2.reshape(n_gather, ns, D), pos.reshape(E, n_gather, 1, ns), w_gate, w_up, w_down)


def _window_base(off_ref, e, i, offw, cap, ws):
    o0 = off_ref[e * offw + i]
    return jnp.minimum((o0 // BF16_ROWS) * BF16_ROWS, cap - ws)


def _combine_kernel(off_ref, x_ref, pos_ref, g_ref, mod_ref, fg_ref, y_hbm, xo_ref, win, extra, sem, xsem,
                    *, n_exp, n_tile, cap, ws, offw, final_norm):
    i = pl.program_id(0)
    slot = i % 2
    tb = x_ref.shape[0]

    def window_copy(ti, sl, e):
        base = _window_base(off_ref, e, ti, offw, cap, ws)
        return pltpu.make_async_copy(y_hbm.at[pl.ds(e * cap + base, ws), :],
                                     win.at[sl, e], sem.at[sl, e])

    @pl.when(i == 0)
    def _():
        for e in range(n_exp):
            window_copy(0, 0, e).start()

    nxt = jnp.minimum(i + 1, n_tile - 1)
    for e in range(n_exp):
        window_copy(nxt, 1 - slot, e).start()

    half = ws // 2
    kw = n_exp * half
    gate_t = jnp.transpose(g_ref[...]).astype(BF16)
    shift = half.bit_length() - 1
    post = jnp.transpose(pos_ref[...].astype(F32)).astype(I32)
    digits = jnp.where(post < 0, NO_SLOT, post)
    col = lax.broadcasted_iota(I32, (n_exp, kw), 1)
    spread = (lax.shift_right_logical(col, shift) == lax.broadcasted_iota(I32, (n_exp, kw), 0)).astype(BF16)
    hi = lax.shift_right_logical(digits, DIGIT_BITS).astype(F32).astype(BF16)
    lo = (digits & ((1 << DIGIT_BITS) - 1)).astype(F32).astype(BF16)
    rel = (float(1 << DIGIT_BITS) * jnp.dot(hi, spread, preferred_element_type=F32)
           + jnp.dot(lo, spread, preferred_element_type=F32))
    col1 = lax.broadcasted_iota(I32, (1, kw), 1)
    owner = lax.shift_right_logical(col1, shift)
    tgt = col1 & (half - 1)
    second = False
    for e in range(n_exp):
        base = _window_base(off_ref, e, i, offw, cap, ws)
        tgt = jnp.where(owner == e, tgt + base, tgt)
        second = jnp.logical_or(second, off_ref[e * offw + i + 1] > base + half)
        window_copy(i, slot, e).wait()
    tgt = tgt.astype(F32)
    weight = jnp.dot(gate_t, spread, preferred_element_type=F32)
    gate2 = mod_ref[0, 5:6, :]
    d_model = x_ref.shape[1]
    hit = jnp.where(rel == tgt, weight, 0.0).astype(BF16)
    acc = jnp.dot(hit, win[slot, :, :half, :].reshape(kw, d_model), preferred_element_type=F32)
    xo_ref[...] = x_ref[...] + gate2 * acc

    @pl.when(second)
    def _():
        hit2 = jnp.where(rel == tgt + float(half), weight, 0.0).astype(BF16)
        xo_ref[...] += gate2 * jnp.dot(hit2, win[slot, :, half:, :].reshape(kw, d_model),
                                       preferred_element_type=F32)

    lane = lax.broadcasted_iota(I32, (tb, ws), 1)

    for e in range(n_exp):
        end = off_ref[e * offw + i + 1]
        base = _window_base(off_ref, e, i, offw, cap, ws)
        n_more = jnp.maximum(end - (base + ws) + ws - 1, 0) // ws

        def more(k, _):
            b2 = jnp.minimum(base + (k + 1) * ws, cap - ws)
            cp = pltpu.make_async_copy(y_hbm.at[pl.ds(e * cap + b2, ws), :], extra, xsem)
            cp.start()
            cp.wait()
            pc = jnp.transpose(pos_ref[...].astype(F32)).astype(I32)[:, e:e + 1]
            h2 = (pc == (b2 + lane)) & (pc >= base + (k + 1) * ws)
            wcol = jnp.transpose(g_ref[...]).astype(BF16).astype(F32)[:, e:e + 1]
            xo_ref[...] += gate2 * jnp.dot(jnp.where(h2, wcol, 0.0).astype(BF16), extra[...],
                                           preferred_element_type=F32)
            return 0

        lax.fori_loop(0, n_more, more, 0)

    if final_norm:
        xo_ref[...] = _rms(xo_ref[...], fg_ref[...])

    @pl.when(i == n_tile - 1)
    def _():
        for e in range(n_exp):
            window_copy(nxt, 1 - slot, e).wait()


def _combine(x2, pos, g, off, y2, mod, tokens_per_batch, cap, tb, ws, final_g):
    N, D = x2.shape
    E = pos.shape[0]
    offw = off.shape[1]
    n_tile = N // tb
    per_b = tokens_per_batch // tb
    bc = mod.shape[0]
    mod_map = (lambda i, off: (i // per_b, 0, 0)) if bc > 1 else (lambda i, off: (0, 0, 0))
    grid_spec = pltpu.PrefetchScalarGridSpec(
        num_scalar_prefetch=1,
        grid=(n_tile,),
        in_specs=[pl.BlockSpec((tb, D), lambda i, off: (i, 0)),
                  pl.BlockSpec((E, tb), lambda i, off: (0, i)),
                  pl.BlockSpec((E, tb), lambda i, off: (0, i)),
                  pl.BlockSpec((1, 6, D), mod_map),
                  pl.BlockSpec((1, D), lambda i, off: (0, 0)),
                  pl.BlockSpec(memory_space=pl.ANY)],
        out_specs=pl.BlockSpec((tb, D), lambda i, off: (i, 0)),
        scratch_shapes=[pltpu.VMEM((2, E, ws, D), BF16), pltpu.VMEM((ws, D), BF16),
                        pltpu.SemaphoreType.DMA((2, E)), pltpu.SemaphoreType.DMA(())],
    )
    return pl.pallas_call(
        functools.partial(_combine_kernel, n_exp=E, n_tile=n_tile, cap=cap, ws=ws, offw=offw,
                          final_norm=final_g is not None),
        grid_spec=grid_spec,
        out_shape=jax.ShapeDtypeStruct((N, D), F32),
        compiler_params=_cparams(("arbitrary",)),
        name="moe_combine",
    )(off.reshape(-1), x2, pos, g, mod, (mod[0, 0] if final_g is None else final_g).reshape(1, D), y2)


def _moe(x, hn2, logits_t, mod, p, layer, final_g):
    B, T, D = x.shape
    N = B * T
    E = logits_t.shape[0]
    cap = EC_FACTOR * N // E
    tb = min(TOKEN_TILE, T)
    ws = min(SLOT_WINDOW, cap)
    per_expert = (cap + ws) * D * 2
    grp = 1
    while grp * 2 <= E and grp * 2 * per_expert <= STAGING_BYTES:
        grp *= 2
    sb = min(SLOT_BLOCK, cap)
    pos, g, off = _select(logits_t, cap, tb)
    y2 = _experts(hn2.reshape(N, D), pos, off, p["moe_w_gate"], p["moe_w_up"], p["moe_w_down"],
                  layer, cap, tb, sb, ws, grp)
    assert cap <= NO_SLOT // 2
    xo = _combine(x.reshape(N, D), pos, g, off, y2, mod, T, cap, tb, min(COMBINE_WINDOW, cap), final_g)
    return xo.reshape(B, T, D)


def _grid_pos_embed(n_tokens, d_model):
    rows = n_tokens // GRID_W
    row = jnp.repeat(jnp.arange(rows, dtype=F32), GRID_W)
    col = jnp.tile(jnp.arange(GRID_W, dtype=F32), rows)
    q = d_model // 4
    freq = jnp.exp(-math.log(POS_BASE) * jnp.arange(q, dtype=F32) / q)
    ang_r = row[:, None] * freq
    ang_c = col[:, None] * freq
    return jnp.concatenate([jnp.sin(ang_r), jnp.cos(ang_r), jnp.sin(ang_c), jnp.cos(ang_c)], axis=-1)


def _trunk(x, pe, mods, h0, p):
    depth = p["norm1_g"].shape[0]
    finals = []
    zeros = jnp.zeros_like(pe)
    for l in range(depth):
        mod = mods[l]
        router2 = _split_router(p["moe_router"][l])
        j = l // 2
        if l % 2 == 0:
            x, hn2, lg, fin = _lru_layer(x, pe if l == 0 else zeros, mod, h0[:, j], p, j,
                                         p["norm1_g"][l], p["norm2_g"][l], router2)
            finals.append(fin)
        else:
            x, hn2, lg = _sgu_layer(x, mod, p, j, p["norm1_g"][l], p["norm2_g"][l], router2)
        x = _moe(x, hn2, lg, mod, p, l, p["final_norm_g"] if l == depth - 1 else None)
    return x, jnp.stack(finals, axis=1)


def kernel(x_prompt, x_sample, state_lru, c, c_ctx, norm1_g, norm2_g, w_mod, b_mod,
           lru_w_in, lru_conv_w, lru_conv_b, lru_w_a, lru_b_a, lru_w_x, lru_b_x, lru_lam, lru_w_out,
           sg_w_in, sg_norm_g, sg_w_s, sg_b_s, sg_w_out,
           moe_router, moe_w_gate, moe_w_up, moe_w_down, final_norm_g):
    p = dict(norm1_g=norm1_g, norm2_g=norm2_g, lru_w_in=lru_w_in, lru_conv_w=lru_conv_w,
             lru_conv_b=lru_conv_b, lru_w_a=lru_w_a, lru_b_a=lru_b_a, lru_w_x=lru_w_x, lru_b_x=lru_b_x,
             lru_lam=lru_lam, lru_w_out=lru_w_out, sg_w_in=sg_w_in, sg_norm_g=sg_norm_g, sg_w_s=sg_w_s,
             sg_b_s=sg_b_s, sg_w_out=sg_w_out, moe_router=moe_router, moe_w_gate=moe_w_gate.astype(BF16),
             moe_w_up=moe_w_up.astype(BF16), moe_w_down=moe_w_down.astype(BF16), final_norm_g=final_norm_g)
    L, D, _ = w_mod.shape
    bs = c.shape[0]
    n_lru, _, W = lru_lam.shape

    rows = -(-(1 + bs) // SUBLANE) * SUBLANE
    cond = jnp.zeros((rows, D), F32).at[0].set(c_ctx).at[1:1 + bs].set(c)
    mods = _modulation(cond, w_mod, b_mod).reshape(L, rows, 6, D)

    h0_ctx = jnp.zeros((x_prompt.shape[0], n_lru, 2, W), F32)
    y_prompt, new_state = _trunk(x_prompt, jnp.zeros(x_prompt.shape[1:], F32), mods[:, 0:1], h0_ctx, p)
    y_sample, _ = _trunk(x_sample, _grid_pos_embed(x_sample.shape[1], D), mods[:, 1:1 + bs], state_lru, p)
    return (y_prompt, y_sample, new_state)
```

```python
import functools
import math

import jax
import jax.numpy as jnp
from jax import lax
from jax.experimental import pallas as pl
from jax.experimental.pallas import tpu as pltpu

F32 = jnp.float32
BF16 = jnp.bfloat16
I32 = jnp.int32
HIGHEST = lax.Precision.HIGHEST

RMS_EPS = 1e-6
F32_TINY = 1e-30
LRU_C = 8.0
CONV_W = 4
CHUNK = 128
SG_GROUPS = 8
GRID_W = 64
POS_BASE = 10000.0
EC_FACTOR = 2
LANE = 128
SUBLANE = 8
BF16_ROWS = 16
VMEM_LIMIT = 56 * 1024 * 1024

TOKEN_TILE = 256
SLOT_BLOCK = 1024
GATHER_TOKENS = 2048
LRU_ROWS = 512
SGU_ROWS = 512
STAGING_BYTES = 22 * 1024 * 1024
COMBINE_WINDOW = 128
DIGIT_BITS = 6
NO_SLOT = 1 << 14
SLOT_WINDOW = 64


def _cparams(sem):
    return pltpu.CompilerParams(dimension_semantics=sem, vmem_limit_bytes=VMEM_LIMIT)


def _rms(x, g):
    return x * lax.rsqrt(jnp.mean(x * x, axis=-1, keepdims=True) + RMS_EPS) * g


def _gelu_tanh(x):
    c = math.sqrt(2.0 / math.pi)
    hx = 0.5 * x
    return hx + hx * jnp.tanh(x * (c + (c * 0.044715) * (x * x)))


def _sigmoid(x):
    return 1.0 / (1.0 + jnp.exp(-x))


def _log1p(e):
    w = 1.0 + e
    return jnp.where(w == 1.0, e, e * jnp.log(w) / jnp.where(w == 1.0, 1.0, w - 1.0))


def _softplus(x):
    return jnp.maximum(x, 0.0) + _log1p(jnp.exp(-jnp.abs(x)))


def _sqrt_neg_expm1_2x(x):
    t = jnp.tanh(x)
    m = -2.0 * t
    return m * lax.rsqrt(jnp.maximum(m * (1.0 - t), F32_TINY))


def _mod_kernel(c_ref, w_ref, b_ref, o_ref):
    c = c_ref[...]
    sc = c * _sigmoid(c)
    o_ref[0] = jnp.dot(sc, w_ref[0], preferred_element_type=F32, precision=HIGHEST) + b_ref[0]


def _modulation(cond, w_mod, b_mod):
    L, D, D6 = w_mod.shape
    R = cond.shape[0]
    tn = D6 // 4
    return pl.pallas_call(
        _mod_kernel,
        grid=(L, D6 // tn),
        in_specs=[pl.BlockSpec((R, D), lambda l, n: (0, 0)),
                  pl.BlockSpec((1, D, tn), lambda l, n: (l, 0, n)),
                  pl.BlockSpec((1, 1, tn), lambda l, n: (l, 0, n))],
        out_specs=pl.BlockSpec((1, R, tn), lambda l, n: (l, 0, n)),
        out_shape=jax.ShapeDtypeStruct((L, R, D6), F32),
        compiler_params=_cparams(("arbitrary", "arbitrary")),
        name="modulation",
    )(cond, w_mod, b_mod.reshape(L, 1, D6))


def _lru_gates(xb, wg_ref, ba, bx, sp, a_scr, u_scr):
    heads = wg_ref.shape[0]
    for h in range(heads):
        hs = slice(h * LANE, (h + 1) * LANE)
        xh = xb[:, hs]
        z = jnp.dot(xh.astype(BF16), wg_ref[h], preferred_element_type=F32)
        tr = jnp.tanh(z[:, :LANE] + ba[:, hs])
        ti = jnp.tanh(z[:, LANE:] + bx[:, hs])
        log_a = sp[:, hs] + sp[:, hs] * tr
        a = jnp.exp(log_a)
        hx = 0.5 * xh
        u = _sqrt_neg_expm1_2x(log_a) * (hx + hx * ti)
        a_scr[:, hs] = a
        u_scr[:, hs] = u


def _lru_fwd_kernel(x_ref, xn_ref, pe_ref, pen_ref, mod_ref, g_ref, win_ref, cw_ref, cb_ref, wg_ref,
                    ba_ref, bx_ref, lam_ref, h0_ref,
                    gate_ref, xb_ref, hf_ref, fin_ref,
                    carry_ref, prev_scr, a_scr, u_scr, *, tt, n_t, width):
    t = pl.program_id(0)
    nb = x_ref.shape[0]
    rows = tt * nb

    @pl.when(t == 0)
    def _():
        carry_ref[...] = h0_ref[...]
        prev_scr[...] = jnp.zeros(prev_scr.shape, F32)

    ext = jnp.concatenate([jnp.swapaxes(x_ref[...], 0, 1) + pe_ref[...][:, None, :],
                           (xn_ref[:, 0, :] + pen_ref[0:1, :])[None]], axis=0)
    hn = _rms(ext, g_ref[...]) * (1.0 + mod_ref[:, 1, :][None]) + mod_ref[:, 0, :][None]
    hn = hn.reshape(rows + nb, hn.shape[-1]).astype(BF16)
    proj = jnp.dot(hn, win_ref[...], preferred_element_type=F32)
    gate_ref[0] = _gelu_tanh(proj[:rows, :width])

    ahead = jnp.where(t < n_t - 1, proj[rows:, width:], 0.0)
    span = jnp.concatenate([prev_scr[...], proj[:rows, width:], ahead], axis=0)
    prev_scr[...] = proj[rows - 2 * nb:rows, width:]
    xb = cb_ref[...]
    for k in range(CONV_W):
        xb = xb + cw_ref[k:k + 1, :] * span[k * nb:k * nb + rows, :]
    xb_ref[0] = xb

    sp = (-0.5 * LRU_C) * _softplus(-lam_ref[...])
    _lru_gates(xb, wg_ref, ba_ref[...], bx_ref[...], sp, a_scr, u_scr)

    def body(ti, h):
        r0 = pl.multiple_of(ti * nb, nb)
        h = u_scr[pl.ds(r0, nb), :] + a_scr[pl.ds(r0, nb), :] * h
        hf_ref[0, pl.ds(r0, nb), :] = h
        return h

    h = lax.fori_loop(0, tt, body, carry_ref[...])
    carry_ref[...] = h
    fin_ref[...] = h


def _lru_bwd_kernel(x_ref, pe_ref, gate_ref, xb_ref, hf_ref, mod_ref, wg_ref, ba_ref, bx_ref, lam_ref, h0_ref,
                    wout_ref, g2_ref, r_ref,
                    xo_ref, hn_ref, lg_ref, fin_ref,
                    carry_ref, a_scr, u_scr, hb_scr, *, tt):
    t = pl.program_id(0)
    nb = x_ref.shape[0]

    @pl.when(t == 0)
    def _():
        carry_ref[...] = h0_ref[...]

    sp = (-0.5 * LRU_C) * _softplus(-lam_ref[...])
    _lru_gates(xb_ref[0], wg_ref, ba_ref[...], bx_ref[...], sp, a_scr, u_scr)

    def body(ti, h):
        r0 = pl.multiple_of((tt - 1 - ti) * nb, nb)
        h = u_scr[pl.ds(r0, nb), :] + a_scr[pl.ds(r0, nb), :] * h
        hb_scr[pl.ds(r0, nb), :] = h
        return h

    h = lax.fori_loop(0, tt, body, carry_ref[...])
    carry_ref[...] = h
    fin_ref[...] = h

    y = ((hf_ref[0] + hb_scr[...]) * gate_ref[0]).astype(BF16)
    y = jnp.dot(y, wout_ref[...], preferred_element_type=F32)
    y = jnp.swapaxes(y.reshape(tt, nb, y.shape[-1]), 0, 1)
    xo = (x_ref[...] + pe_ref[...][None]) + mod_ref[:, 2:3, :] * y
    xo_ref[...] = xo
    hn = _rms(xo, g2_ref[...]) * (1.0 + mod_ref[:, 4:5, :]) + mod_ref[:, 3:4, :]
    hn_ref[...] = hn.astype(BF16)
    lg = _router_logits(hn.reshape(nb * tt, hn.shape[-1]), r_ref)
    lg_ref[...] = lg.reshape(nb, tt, lg.shape[-1])


def _router_logits(hn, r_ref):
    h_hi = hn.astype(BF16)
    h_lo = (hn - h_hi.astype(F32)).astype(BF16)
    return (jnp.dot(h_hi, r_ref[0], preferred_element_type=F32)
            + jnp.dot(h_hi, r_ref[1], preferred_element_type=F32)
            + jnp.dot(h_lo, r_ref[0], preferred_element_type=F32))


def _split_router(router):
    hi = router.astype(BF16)
    lo = (router - hi.astype(F32)).astype(BF16)
    return jnp.stack([hi, lo])


def _moe_prenorm(xo, mod_ref, g2_ref, r_ref, hn_ref, lg_ref):
    hn = _rms(xo, g2_ref[...]) * (1.0 + mod_ref[0, 4:5, :]) + mod_ref[0, 3:4, :]
    hn_ref[0] = hn.astype(BF16)
    lg_ref[0] = _router_logits(hn, r_ref)


def _lru_layer(x, pe, mod, h0, p, j, norm1_g, norm2_g, router2):
    B, T, D = x.shape
    W = p["lru_lam"].shape[-1]
    H = W // LANE
    E = router2.shape[-1]
    tt = min(T, max(SUBLANE, LRU_ROWS // B))
    n_t = T // tt
    rows = tt * B
    bc = mod.shape[0]
    assert B % SUBLANE == 0 and T % tt == 0 and tt % SUBLANE == 0 and bc in (1, B)
    n8 = T // SUBLANE
    r8 = tt // SUBLANE

    def wg(d):
        return (0.5 * jnp.concatenate([p["lru_w_a"][j, d], p["lru_w_x"][j, d]], axis=-1)).astype(BF16)

    def vec(name, d, scale=1.0):
        return (scale * p[name][j, d]).reshape(1, W)

    full = lambda shape: pl.BlockSpec(shape, lambda t: (0,) * len(shape), pipeline_mode=pl.Buffered(1))
    fwd_x = lambda t: (0, t, 0)
    fwd_w = lambda t: (t, 0, 0)
    tile_w = lambda imap: pl.BlockSpec((1, rows, W), imap)
    inner = jax.ShapeDtypeStruct((n_t, rows, W), F32)
    gate, xb, hf, fin_f = pl.pallas_call(
        functools.partial(_lru_fwd_kernel, tt=tt, n_t=n_t, width=W),
        grid=(n_t,),
        in_specs=[pl.BlockSpec((B, tt, D), fwd_x),
                  pl.BlockSpec((B, SUBLANE, D), lambda t: (0, jnp.minimum((t + 1) * r8, n8 - 1), 0)),
                  pl.BlockSpec((tt, D), lambda t: (t, 0)),
                  pl.BlockSpec((SUBLANE, D), lambda t: (jnp.minimum((t + 1) * r8, n8 - 1), 0)),
                  full((bc, 6, D)),
                  full((1, D)), full((D, 2 * W)), full((CONV_W, W)), full((1, W)),
                  full((H, LANE, 2 * LANE)), full((1, W)), full((1, W)), full((1, W)),
                  full((B, W))],
        out_specs=[tile_w(fwd_w), tile_w(fwd_w), tile_w(fwd_w), pl.BlockSpec((B, W), lambda t: (0, 0))],
        out_shape=[inner] * 3 + [jax.ShapeDtypeStruct((B, W), F32)],
        scratch_shapes=[pltpu.VMEM((B, W), F32), pltpu.VMEM((2 * B, W), F32),
                        pltpu.VMEM((rows, W), F32), pltpu.VMEM((rows, W), F32)],
        compiler_params=_cparams(("arbitrary",)),
        name="lru_fwd",
    )(x, x, pe, pe, mod, norm1_g.reshape(1, D), p["lru_w_in"][j].astype(BF16), p["lru_conv_w"][j],
      p["lru_conv_b"][j].reshape(1, W), wg(0), vec("lru_b_a", 0, 0.5), vec("lru_b_x", 0, 0.5), vec("lru_lam", 0),
      h0[:, 0])

    bwd_x = lambda t: (0, n_t - 1 - t, 0)
    bwd_w = lambda t: (n_t - 1 - t, 0, 0)
    xo, hn2, lg, fin_b = pl.pallas_call(
        functools.partial(_lru_bwd_kernel, tt=tt),
        grid=(n_t,),
        in_specs=[pl.BlockSpec((B, tt, D), bwd_x), pl.BlockSpec((tt, D), lambda t: (n_t - 1 - t, 0)),
                  tile_w(bwd_w), tile_w(bwd_w), tile_w(bwd_w),
                  full((bc, 6, D)),
                  full((H, LANE, 2 * LANE)), full((1, W)), full((1, W)), full((1, W)),
                  full((B, W)),
                  full((W, D)), full((1, D)), full((2, D, E))],
        out_specs=[pl.BlockSpec((B, tt, D), bwd_x), pl.BlockSpec((B, tt, D), bwd_x),
                   pl.BlockSpec((B, tt, E), bwd_x), pl.BlockSpec((B, W), lambda t: (0, 0))],
        out_shape=[jax.ShapeDtypeStruct((B, T, D), F32), jax.ShapeDtypeStruct((B, T, D), BF16),
                   jax.ShapeDtypeStruct((B, T, E), F32), jax.ShapeDtypeStruct((B, W), F32)],
        scratch_shapes=[pltpu.VMEM((B, W), F32), pltpu.VMEM((rows, W), F32), pltpu.VMEM((rows, W), F32),
                        pltpu.VMEM((rows, W), F32)],
        compiler_params=_cparams(("arbitrary",)),
        name="lru_bwd",
    )(x, pe, gate, xb, hf, mod, wg(1), vec("lru_b_a", 1, 0.5), vec("lru_b_x", 1, 0.5), vec("lru_lam", 1), h0[:, 1],
      p["lru_w_out"][j].astype(BF16), norm2_g.reshape(1, D), router2)
    return xo, hn2, lg.reshape(B * T, E).T, jnp.stack([fin_f, fin_b], axis=1)


def _sgu_kernel(x_ref, mod_ref, g_ref, win_ref, ng_ref, ws_ref, bs_ref, wout_ref, g2_ref, rt_ref,
                xo_ref, hn_ref, lg_ref, u_scr, v_scr, p_scr, *, tt, sgw):
    x = x_ref[0]
    hn = (_rms(x, g_ref[...]) * (1.0 + mod_ref[0, 1:2, :]) + mod_ref[0, 0:1, :]).astype(BF16)
    u_scr[...] = _gelu_tanh(jnp.dot(hn, win_ref[:, :sgw], preferred_element_type=F32))
    v = _gelu_tanh(jnp.dot(hn, win_ref[:, sgw:], preferred_element_type=F32))
    v_scr[...] = _rms(v, ng_ref[...]).astype(BF16)
    gd = sgw // SG_GROUPS
    for n in range(tt // CHUNK):
        rs = slice(n * CHUNK, (n + 1) * CHUNK)
        for g in range(SG_GROUPS):
            cs = slice(g * gd, (g + 1) * gd)
            sv = jnp.dot(ws_ref[g], v_scr[rs, cs], preferred_element_type=F32) + bs_ref[:, g:g + 1]
            p_scr[rs, cs] = (u_scr[rs, cs] * sv).astype(BF16)
    y = jnp.dot(p_scr[...], wout_ref[...], preferred_element_type=F32)
    xo = x + mod_ref[0, 2:3, :] * y
    xo_ref[0] = xo
    _moe_prenorm(xo, mod_ref, g2_ref, rt_ref, hn_ref, lg_ref)


def _sgu_layer(x, mod, p, j, norm1_g, norm2_g, router2):
    B, T, D = x.shape
    sgw = p["sg_norm_g"].shape[-1]
    E = router2.shape[-1]
    tt = min(T, SGU_ROWS)
    n_t = T // tt
    bc = mod.shape[0]
    mod_map = (lambda b, t: (b, 0, 0)) if bc > 1 else (lambda b, t: (0, 0, 0))
    full = lambda shape: pl.BlockSpec(shape, lambda b, t: (0,) * len(shape), pipeline_mode=pl.Buffered(1))
    tile = pl.BlockSpec((1, tt, D), lambda b, t: (b, t, 0))
    xo, hn2, lg = pl.pallas_call(
        functools.partial(_sgu_kernel, tt=tt, sgw=sgw),
        grid=(B, n_t),
        in_specs=[tile, pl.BlockSpec((1, 6, D), mod_map), full((1, D)), full((D, 2 * sgw)), full((1, sgw)),
                  full((SG_GROUPS, CHUNK, CHUNK)), full((CHUNK, SG_GROUPS)), full((sgw, D)),
                  full((1, D)), full((2, D, E))],
        out_specs=[tile, tile, pl.BlockSpec((1, tt, E), lambda b, t: (b, t, 0))],
        out_shape=[jax.ShapeDtypeStruct((B, T, D), F32), jax.ShapeDtypeStruct((B, T, D), BF16),
                   jax.ShapeDtypeStruct((B, T, E), F32)],
        scratch_shapes=[pltpu.VMEM((tt, sgw), F32), pltpu.VMEM((tt, sgw), BF16), pltpu.VMEM((tt, sgw), BF16)],
        compiler_params=_cparams(("arbitrary", "arbitrary")),
        name="sgu",
    )(x, mod, norm1_g.reshape(1, D), p["sg_w_in"][j].astype(BF16), p["sg_norm_g"][j].reshape(1, sgw),
      p["sg_w_s"][j].astype(BF16), p["sg_b_s"][j].T, p["sg_w_out"][j].astype(BF16),
      norm2_g.reshape(1, D), router2)
    return xo, hn2, lg.reshape(B * T, E).T


def _select_kernel(lg_ref, pos_ref, g_ref, off_ref, aff_scr, *, cap, tb):
    E, N = lg_ref.shape
    n_tile = N // tb
    lg = lg_ref[...]
    ex = jnp.exp(lg - jnp.max(lg, axis=0, keepdims=True))
    aff_scr[...] = ex / jnp.sum(ex, axis=0, keepdims=True)

    def search(it, cur):
        cand = cur | (1 << (30 - it))
        bits = pltpu.bitcast(aff_scr[...], I32)
        cnt = jnp.sum(jnp.where(bits >= cand, 1.0, 0.0), axis=1, keepdims=True)
        return jnp.where(cnt >= cap, cand, cur)

    thr = lax.fori_loop(0, 31, search, jnp.zeros((E, 1), I32))
    bits = pltpu.bitcast(aff_scr[...], I32)
    n_gt = jnp.sum(jnp.where(bits > thr, 1.0, 0.0), axis=1, keepdims=True)
    need = cap - n_gt

    tri = (lax.broadcasted_iota(I32, (tb, tb), 0) <= lax.broadcasted_iota(I32, (tb, tb), 1)).astype(BF16)
    lane = lax.broadcasted_iota(I32, off_ref.shape, 1)

    def chunk(c, carry):
        c_eq, c_pos = carry
        l0 = pl.multiple_of(c * tb, tb)
        aff = aff_scr[:, pl.ds(l0, tb)]
        b = pltpu.bitcast(aff, I32)
        eq = b == thr
        eqf = jnp.where(eq, 1.0, 0.0)
        rank = jnp.dot(eqf.astype(BF16), tri, preferred_element_type=F32) - eqf + c_eq
        sel = (b > thr) | (eq & (rank < need))
        self_ = jnp.where(sel, 1.0, 0.0)
        inc = jnp.dot(self_.astype(BF16), tri, preferred_element_type=F32)
        pos = inc - self_ + c_pos
        pos_ref[:, pl.ds(l0, tb)] = jnp.where(sel, pos, -1.0).astype(I32)
        g_ref[:, pl.ds(l0, tb)] = jnp.where(sel, aff, 0.0)
        off_ref[...] = jnp.where(lane == c, jnp.broadcast_to(c_pos, off_ref.shape).astype(I32), off_ref[...])
        return (c_eq + jnp.sum(eqf, axis=1, keepdims=True), c_pos + inc[:, tb - 1:tb])

    off_ref[...] = jnp.zeros(off_ref.shape, I32)
    zero = jnp.zeros((E, 1), F32)
    _, total = lax.fori_loop(0, n_tile, chunk, (zero, zero))
    off_ref[...] = jnp.where(lane == n_tile, jnp.broadcast_to(total, off_ref.shape).astype(I32), off_ref[...])


def _select(logits_t, cap, tb):
    E, N = logits_t.shape
    offw = -(-(N // tb + 1) // LANE) * LANE
    return pl.pallas_call(
        functools.partial(_select_kernel, cap=cap, tb=tb),
        out_shape=[jax.ShapeDtypeStruct((E, N), I32), jax.ShapeDtypeStruct((E, N), F32),
                   jax.ShapeDtypeStruct((E, offw), I32)],
        scratch_shapes=[pltpu.VMEM((E, N), F32)],
        compiler_params=pltpu.CompilerParams(vmem_limit_bytes=VMEM_LIMIT),
        name="moe_select",
    )(logits_t)


def _expert_kernel(off_ref, x_ref, pos_ref, wg_ref, wu_ref, wd_ref, y_ref,
                   xs_scr, *, n_gather, n_blk, tb, sb, ws, offw):
    s = pl.program_id(0)
    d = pl.program_id(1)
    sub = x_ref.shape[1] // tb
    grp = xs_scr.shape[0]
    row_iota = lax.broadcasted_iota(I32, (ws, tb), 0)

    @pl.when(d == 0)
    def _():
        xs_scr[...] = jnp.zeros(xs_scr.shape, xs_scr.dtype)

    @pl.when(d < n_gather)
    def _gather():
        for j in range(sub):
            i = d * sub + j
            xt = x_ref[0, j * tb:(j + 1) * tb, :]
            lanes = slice(j * tb, (j + 1) * tb)
            bases, hits = [], []
            for k in range(grp):
                e = s * grp + k
                o0 = off_ref[e * offw + i]
                base = pl.multiple_of((o0 // BF16_ROWS) * BF16_ROWS, BF16_ROWS)
                bases.append(base)
                hits.append(pos_ref[k, 0, :, lanes] == (base + row_iota))
            stacked = jnp.where(jnp.concatenate(hits, axis=0), 1.0, 0.0).astype(BF16)
            rows = jnp.dot(stacked, xt, preferred_element_type=F32)
            for k in range(grp):
                xs_scr[k, pl.ds(bases[k], ws), :] += rows[k * ws:(k + 1) * ws, :].astype(BF16)
            for k in range(grp):
                e = s * grp + k
                end = off_ref[e * offw + i + 1]
                n_more = jnp.maximum(end - (bases[k] + ws) + ws - 1, 0) // ws

                def more(ch, _):
                    r0 = pl.multiple_of(bases[k] + (ch + 1) * ws, BF16_ROWS)
                    hit = pos_ref[k, 0, :, lanes] == (r0 + row_iota)
                    extra = jnp.dot(jnp.where(hit, 1.0, 0.0).astype(BF16), xt, preferred_element_type=F32)
                    xs_scr[k, pl.ds(r0, ws), :] += extra.astype(BF16)
                    return 0

                lax.fori_loop(0, n_more, more, 0)

    @pl.when(d >= n_gather)
    def _ffn():
        step = d - n_gather
        k = step // n_blk
        r0 = pl.multiple_of((step - k * n_blk) * sb, sb)
        xb = xs_scr[k, pl.ds(r0, sb), :]
        hg = jnp.dot(xb, wg_ref[0, 0], preferred_element_type=F32)
        hu = jnp.dot(xb, wu_ref[0, 0], preferred_element_type=F32)
        h = (hg * _sigmoid(hg) * hu).astype(BF16)
        o = jnp.dot(h, wd_ref[0, 0], preferred_element_type=F32)
        y_ref[...] = o.astype(BF16)


def _experts(hn2, pos, off, w_gate, w_up, w_down, layer, cap, tb, sb, ws, grp):
    N, D = hn2.shape
    E = pos.shape[0]
    F = w_gate.shape[-1]
    offw = off.shape[1]
    n_blk = cap // sb
    ns = min(N, GATHER_TOKENS)
    n_gather = N // ns
    assert ns % tb == 0 and cap % sb == 0 and N % ns == 0 and E % grp == 0
    tok = lambda s, d, off: jnp.minimum(d, n_gather - 1)
    ffn = lambda d: jnp.maximum(d - n_gather, 0)
    emap = lambda s, d, off: (s, tok(s, d, off), 0, 0)
    wmap = lambda s, d, off: (layer, s * grp + ffn(d) // n_blk, 0, 0)
    grid_spec = pltpu.PrefetchScalarGridSpec(
        num_scalar_prefetch=1,
        grid=(E // grp, n_gather + grp * n_blk),
        in_specs=[pl.BlockSpec((1, ns, D), lambda s, d, off: (tok(s, d, off), 0, 0)),
                  pl.BlockSpec((grp, 1, 1, ns), emap),
                  pl.BlockSpec((1, 1, D, F), wmap), pl.BlockSpec((1, 1, D, F), wmap),
                  pl.BlockSpec((1, 1, F, D), wmap)],
        out_specs=pl.BlockSpec((sb, D), lambda s, d, off: (s * grp * n_blk + ffn(d), 0)),
        scratch_shapes=[pltpu.VMEM((grp, cap + ws, D), BF16)],
    )
    return pl.pallas_call(
        functools.partial(_expert_kernel, n_gather=n_gather, n_blk=n_blk, tb=tb, sb=sb, ws=ws, offw=offw),
        grid_spec=grid_spec,
        out_shape=jax.ShapeDtypeStruct((E * cap, D), BF16),
        compiler_params=_cparams(("arbitrary", "arbitrary")),
        name="moe_experts",
    )(off.reshape(-1), hn2.reshape(n_gather, ns, D), pos.reshape(E, n_gather, 1, ns), w_gate, w_up, w_down)


def _window_base(off_ref, e, i, offw, cap, ws):
    o0 = off_ref[e * offw + i]
    return jnp.minimum((o0 // BF16_ROWS) * BF16_ROWS, cap - ws)


def _combine_kernel(off_ref, x_ref, pos_ref, g_ref, mod_ref, fg_ref, y_hbm, xo_ref, win, extra, sem, xsem,
                    *, n_exp, n_tile, cap, ws, offw, final_norm):
    i = pl.program_id(0)
    slot = i % 2
    tb = x_ref.shape[0]

    def window_copy(ti, sl, e):
        base = _window_base(off_ref, e, ti, offw, cap, ws)
        return pltpu.make_async_copy(y_hbm.at[pl.ds(e * cap + base, ws), :],
                                     win.at[sl, e], sem.at[sl, e])

    @pl.when(i == 0)
    def _():
        for e in range(n_exp):
            window_copy(0, 0, e).start(priority=e % 2)

    half = ws // 2
    kw = n_exp * half
    gate_t = jnp.transpose(g_ref[...]).astype(BF16)
    shift = half.bit_length() - 1
    post = jnp.transpose(pos_ref[...].astype(F32)).astype(I32)
    digits = jnp.where(post < 0, NO_SLOT, post)
    col = lax.broadcasted_iota(I32, (n_exp, kw), 1)
    spread = (lax.shift_right_logical(col, shift) == lax.broadcasted_iota(I32, (n_exp, kw), 0)).astype(BF16)
    hi = lax.shift_right_logical(digits, DIGIT_BITS).astype(F32).astype(BF16)
    lo = (digits & ((1 << DIGIT_BITS) - 1)).astype(F32).astype(BF16)
    rel = (float(1 << DIGIT_BITS) * jnp.dot(hi, spread, preferred_element_type=F32)
           + jnp.dot(lo, spread, preferred_element_type=F32))
    weight = jnp.dot(gate_t, spread, preferred_element_type=F32)

    nxt = jnp.minimum(i + 1, n_tile - 1)
    for e in range(n_exp):
        window_copy(nxt, 1 - slot, e).start(priority=e % 2)

    col1 = lax.broadcasted_iota(I32, (1, kw), 1)
    owner = lax.shift_right_logical(col1, shift)
    tgt = col1 & (half - 1)
    second = False
    for e in range(n_exp):
        base = _window_base(off_ref, e, i, offw, cap, ws)
        tgt = jnp.where(owner == e, tgt + base, tgt)
        second = jnp.logical_or(second, off_ref[e * offw + i + 1] > base + half)
        window_copy(i, slot, e).wait()
    tgt = tgt.astype(F32)
    gate2 = mod_ref[0, 5:6, :]
    d_model = x_ref.shape[1]
    hit = jnp.where(rel == tgt, weight, 0.0).astype(BF16)
    acc = jnp.dot(hit, win[slot, :, :half, :].reshape(kw, d_model), preferred_element_type=F32)
    xo_ref[...] = x_ref[...] + gate2 * acc

    @pl.when(second)
    def _():
        hit2 = jnp.where(rel == tgt + float(half), weight, 0.0).astype(BF16)
        xo_ref[...] += gate2 * jnp.dot(hit2, win[slot, :, half:, :].reshape(kw, d_model),
                                       preferred_element_type=F32)

    lane = lax.broadcasted_iota(I32, (tb, ws), 1)

    for e in range(n_exp):
        end = off_ref[e * offw + i + 1]
        base = _window_base(off_ref, e, i, offw, cap, ws)
        n_more = jnp.maximum(end - (base + ws) + ws - 1, 0) // ws

        def more(k, _):
            b2 = jnp.minimum(base + (k + 1) * ws, cap - ws)
            cp = pltpu.make_async_copy(y_hbm.at[pl.ds(e * cap + b2, ws), :], extra, xsem)
            cp.start()
            cp.wait()
            pc = jnp.transpose(pos_ref[...].astype(F32)).astype(I32)[:, e:e + 1]
            h2 = (pc == (b2 + lane)) & (pc >= base + (k + 1) * ws)
            wcol = jnp.transpose(g_ref[...]).astype(BF16).astype(F32)[:, e:e + 1]
            xo_ref[...] += gate2 * jnp.dot(jnp.where(h2, wcol, 0.0).astype(BF16), extra[...],
                                           preferred_element_type=F32)
            return 0

        lax.fori_loop(0, n_more, more, 0)

    if final_norm:
        xo_ref[...] = _rms(xo_ref[...], fg_ref[...])

    @pl.when(i == n_tile - 1)
    def _():
        for e in range(n_exp):
            window_copy(nxt, 1 - slot, e).wait()


def _combine(x2, pos, g, off, y2, mod, tokens_per_batch, cap, tb, ws, final_g):
    N, D = x2.shape
    E = pos.shape[0]
    offw = off.shape[1]
    n_tile = N // tb
    per_b = tokens_per_batch // tb
    bc = mod.shape[0]
    mod_map = (lambda i, off: (i // per_b, 0, 0)) if bc > 1 else (lambda i, off: (0, 0, 0))
    grid_spec = pltpu.PrefetchScalarGridSpec(
        num_scalar_prefetch=1,
        grid=(n_tile,),
        in_specs=[pl.BlockSpec((tb, D), lambda i, off: (i, 0)),
                  pl.BlockSpec((E, tb), lambda i, off: (0, i)),
                  pl.BlockSpec((E, tb), lambda i, off: (0, i)),
                  pl.BlockSpec((1, 6, D), mod_map),
                  pl.BlockSpec((1, D), lambda i, off: (0, 0)),
                  pl.BlockSpec(memory_space=pl.ANY)],
        out_specs=pl.BlockSpec((tb, D), lambda i, off: (i, 0)),
        scratch_shapes=[pltpu.VMEM((2, E, ws, D), BF16), pltpu.VMEM((ws, D), BF16),
                        pltpu.SemaphoreType.DMA((2, E)), pltpu.SemaphoreType.DMA(())],
    )
    return pl.pallas_call(
        functools.partial(_combine_kernel, n_exp=E, n_tile=n_tile, cap=cap, ws=ws, offw=offw,
                          final_norm=final_g is not None),
        grid_spec=grid_spec,
        out_shape=jax.ShapeDtypeStruct((N, D), F32),
        compiler_params=_cparams(("arbitrary",)),
        name="moe_combine",
    )(off.reshape(-1), x2, pos, g, mod, (mod[0, 0] if final_g is None else final_g).reshape(1, D), y2)


def _moe(x, hn2, logits_t, mod, p, layer, final_g):
    B, T, D = x.shape
    N = B * T
    E = logits_t.shape[0]
    cap = EC_FACTOR * N // E
    tb = min(TOKEN_TILE, T)
    ws = min(SLOT_WINDOW, cap)
    per_expert = (cap + ws) * D * 2
    grp = 1
    while grp * 2 <= E and grp * 2 * per_expert <= STAGING_BYTES:
        grp *= 2
    sb = min(SLOT_BLOCK, cap)
    pos, g, off = _select(logits_t, cap, tb)
    y2 = _experts(hn2.reshape(N, D), pos, off, p["moe_w_gate"], p["moe_w_up"], p["moe_w_down"],
                  layer, cap, tb, sb, ws, grp)
    assert cap <= NO_SLOT // 2
    xo = _combine(x.reshape(N, D), pos, g, off, y2, mod, T, cap, tb, min(COMBINE_WINDOW, cap), final_g)
    return xo.reshape(B, T, D)


def _grid_pos_embed(n_tokens, d_model):
    rows = n_tokens // GRID_W
    row = jnp.repeat(jnp.arange(rows, dtype=F32), GRID_W)
    col = jnp.tile(jnp.arange(GRID_W, dtype=F32), rows)
    q = d_model // 4
    freq = jnp.exp(-math.log(POS_BASE) * jnp.arange(q, dtype=F32) / q)
    ang_r = row[:, None] * freq
    ang_c = col[:, None] * freq
    return jnp.concatenate([jnp.sin(ang_r), jnp.cos(ang_r), jnp.sin(ang_c), jnp.cos(ang_c)], axis=-1)


def _trunk(x, pe, mods, h0, p):
    depth = p["norm1_g"].shape[0]
    finals = []
    zeros = jnp.zeros_like(pe)
    for l in range(depth):
        mod = mods[l]
        router2 = _split_router(p["moe_router"][l])
        j = l // 2
        if l % 2 == 0:
            x, hn2, lg, fin = _lru_layer(x, pe if l == 0 else zeros, mod, h0[:, j], p, j,
                                         p["norm1_g"][l], p["norm2_g"][l], router2)
            finals.append(fin)
        else:
            x, hn2, lg = _sgu_layer(x, mod, p, j, p["norm1_g"][l], p["norm2_g"][l], router2)
        x = _moe(x, hn2, lg, mod, p, l, p["final_norm_g"] if l == depth - 1 else None)
    return x, jnp.stack(finals, axis=1)


def kernel(x_prompt, x_sample, state_lru, c, c_ctx, norm1_g, norm2_g, w_mod, b_mod,
           lru_w_in, lru_conv_w, lru_conv_b, lru_w_a, lru_b_a, lru_w_x, lru_b_x, lru_lam, lru_w_out,
           sg_w_in, sg_norm_g, sg_w_s, sg_b_s, sg_w_out,
           moe_router, moe_w_gate, moe_w_up, moe_w_down, final_norm_g):
    p = dict(norm1_g=norm1_g, norm2_g=norm2_g, lru_w_in=lru_w_in, lru_conv_w=lru_conv_w,
             lru_conv_b=lru_conv_b, lru_w_a=lru_w_a, lru_b_a=lru_b_a, lru_w_x=lru_w_x, lru_b_x=lru_b_x,
             lru_lam=lru_lam, lru_w_out=lru_w_out, sg_w_in=sg_w_in, sg_norm_g=sg_norm_g, sg_w_s=sg_w_s,
             sg_b_s=sg_b_s, sg_w_out=sg_w_out, moe_router=moe_router, moe_w_gate=moe_w_gate.astype(BF16),
             moe_w_up=moe_w_up.astype(BF16), moe_w_down=moe_w_down.astype(BF16), final_norm_g=final_norm_g)
    L, D, _ = w_mod.shape
    bs = c.shape[0]
    n_lru, _, W = lru_lam.shape

    rows = -(-(1 + bs) // SUBLANE) * SUBLANE
    cond = jnp.zeros((rows, D), F32).at[0].set(c_ctx).at[1:1 + bs].set(c)
    mods = _modulation(cond, w_mod, b_mod).reshape(L, rows, 6, D)

    h0_ctx = jnp.zeros((x_prompt.shape[0], n_lru, 2, W), F32)
    y_prompt, new_state = _trunk(x_prompt, jnp.zeros(x_prompt.shape[1:], F32), mods[:, 0:1], h0_ctx, p)
    y_sample, _ = _trunk(x_sample, _grid_pos_embed(x_sample.shape[1], D), mods[:, 1:1 + bs], state_lru, p)
    return (y_prompt, y_sample, new_state)
```
